```python
import math
import jax, jax.numpy as jnp
from jax import lax
import numpy as np

D_MODEL = 4096
BATCH = 4
SEQ = 2048
DEPTH = 1
DEC_BATCH = 128
DEC_SEQ = 1
PAST_LEN = 16384
PAGE_SIZE = 128

MIX_WIDTH = D_MODEL
POOL_WIDTH = MIX_WIDTH // 2
CONV_WIDTH = MIX_WIDTH - POOL_WIDTH
POOL_WINDOWS = (2, 4, 8, 16)
N_POOL_GROUPS = len(POOL_WINDOWS)
POOL_GROUP = POOL_WIDTH // N_POOL_GROUPS
MAX_WIN = max(POOL_WINDOWS)
POOL_BUF = MAX_WIN - 1
CONV_HEADS = 16
CONV_WIDTH_K = 3
CONV_BUF = CONV_WIDTH_K - 1
IN_COLS = POOL_WIDTH + 3 * CONV_WIDTH
N_EXPERTS = 256
TOP_K = 8
D_EXPERT = max(128, (2048 * D_MODEL // 7168) // 128 * 128)
ROUTED_SCALE = 2.5
BLK = 128
ALPHA = (2.0 * DEPTH) ** 0.25
BETA = (8.0 * DEPTH) ** -0.25
LN_EPS = 1e-5

kernel_name = "hybrid_pool_shortconv_moe_deepnorm_step"


def layer_norm(x, g, b):
    xf = x.astype(jnp.float32)
    mu = jnp.mean(xf, axis=-1, keepdims=True)
    var = jnp.mean(jnp.square(xf - mu), axis=-1, keepdims=True)
    return ((xf - mu) * lax.rsqrt(var + LN_EPS) * g.astype(jnp.float32) + b.astype(jnp.float32)).astype(x.dtype)


def mixer(x, pool_buf, conv_buf, pos0, w_in, w_pool, pool_scale, conv_w, w_o):
    B, T, _ = x.shape
    proj = jnp.einsum('btd,dc->btc', x, w_in)
    u = proj[..., :POOL_WIDTH]
    gB = proj[..., POOL_WIDTH:POOL_WIDTH + CONV_WIDTH]
    gC = proj[..., POOL_WIDTH + CONV_WIDTH:POOL_WIDTH + 2 * CONV_WIDTH]
    h = proj[..., POOL_WIDTH + 2 * CONV_WIDTH:]

    u_full = jnp.concatenate([pool_buf.astype(u.dtype), u], axis=1)
    csum = jnp.concatenate([jnp.zeros((B, 1, POOL_WIDTH), jnp.float32),
                            jnp.cumsum(u_full.astype(jnp.float32), axis=1)], axis=1)
    pos = (pos0 + jnp.arange(T, dtype=jnp.int32)).astype(jnp.float32)
    end = csum[:, POOL_BUF + 1:POOL_BUF + 1 + T]
    means = []
    for g, w in enumerate(POOL_WINDOWS):
        lo, hi = g * POOL_GROUP, (g + 1) * POOL_GROUP
        start = csum[:, POOL_BUF + 1 - w:POOL_BUF + 1 - w + T, lo:hi]
        cnt = jnp.minimum(pos + 1.0, float(w))
        means.append((end[..., lo:hi] - start) / cnt[None, :, None])
    pooled = jnp.stack(means, axis=2) - u.reshape(B, T, N_POOL_GROUPS, POOL_GROUP).astype(jnp.float32)
    pooled = pooled.astype(x.dtype)
    pool_out = jnp.einsum('btgc,gcd->btgd', pooled, w_pool).reshape(B, T, POOL_WIDTH) * pool_scale

    v = gC * h
    v_full = jnp.concatenate([conv_buf.astype(v.dtype), v], axis=1)
    conv = (conv_w[0] * v_full[:, 0:T] + conv_w[1] * v_full[:, 1:T + 1]
            + conv_w[2] * v_full[:, 2:T + 2])
    conv_out = gB * conv

    mixed = jnp.concatenate([pool_out, conv_out], axis=-1)
    out = jnp.einsum('btc,cd->btd', mixed, w_o)
    return out, u_full[:, -POOL_BUF:], v_full[:, -CONV_BUF:]


def moe(x2, l, w_router, router_bias, w_gate, w_up, w_down, ws_gate, ws_up, ws_down):
    M, D = x2.shape
    scores = jax.nn.sigmoid(jnp.einsum('md,de->me', x2, w_router[l]).astype(jnp.float32))
    _, idx = lax.top_k(scores + router_bias[l].astype(jnp.float32), TOP_K)
    sel = jnp.take_along_axis(scores, idx, axis=1)
    gates = sel / jnp.sum(sel, axis=-1, keepdims=True) * ROUTED_SCALE

    A = M * TOP_K
    e_flat = idx.reshape(-1).astype(jnp.int32)
    tok_flat = jnp.arange(A, dtype=jnp.int32) // TOP_K
    g_flat = gates.reshape(-1)
    order = jnp.argsort(e_flat, stable=True)
    e_s, tok_s, g_s = e_flat[order], tok_flat[order], g_flat[order]
    counts = jnp.bincount(e_flat, length=N_EXPERTS).astype(jnp.int32)
    padded = (counts + BLK - 1) // BLK * BLK
    starts = jnp.cumsum(counts) - counts
    pends = jnp.cumsum(padded)
    pstarts = pends - padded
    dest = pstarts[e_s] + (jnp.arange(A, dtype=jnp.int32) - starts[e_s])
    nb = (A + N_EXPERTS * (BLK - 1) + BLK - 1) // BLK
    R = nb * BLK
    row_tok = jnp.full((R,), M, jnp.int32).at[dest].set(tok_s)
    row_gate = jnp.zeros((R,), jnp.float32).at[dest].set(g_s)
    block_exp = jnp.minimum(jnp.searchsorted(pends, jnp.arange(nb, dtype=jnp.int32) * BLK, side='right'),
                            N_EXPERTS - 1).astype(jnp.int32)
    x_pad = jnp.concatenate([x2, jnp.zeros((1, D), x2.dtype)], axis=0)

    def body(acc, blk):
        rows, gts, e = blk
        xb = x_pad[rows]
        hb = jax.nn.silu(xb @ w_gate[l, e]) * (xb @ w_up[l, e])
        yb = (hb @ w_down[l, e]) * gts[:, None].astype(x2.dtype)
        return acc.at[rows].add(yb), None

    acc, _ = lax.scan(body, jnp.zeros((M + 1, D), x2.dtype),
                      (row_tok.reshape(nb, BLK), row_gate.reshape(nb, BLK), block_exp))
    shared = (jax.nn.silu(x2 @ ws_gate[l]) * (x2 @ ws_up[l])) @ ws_down[l]
    return acc[:M] + shared


def layer(x, pool_buf, conv_buf, pos0, l, w_in, w_pool, pool_scale, conv_w, w_o,
          ln1_g, ln1_b, ln2_g, ln2_b, w_router, router_bias,
          w_gate, w_up, w_down, ws_gate, ws_up, ws_down):
    mix, new_pool, new_conv = mixer(x, pool_buf, conv_buf, pos0, w_in[l], w_pool[l],
                                    pool_scale[l], conv_w[l], w_o[l])
    h = layer_norm(ALPHA * x + mix, ln1_g[l], ln1_b[l])
    B, T, D = h.shape
    f = moe(h.reshape(B * T, D), l, w_router, router_bias, w_gate, w_up, w_down,
            ws_gate, ws_up, ws_down).reshape(B, T, D)
    y = layer_norm(ALPHA * h + f, ln2_g[l], ln2_b[l])
    return y, new_pool, new_conv


def setup_inputs(seed: int = 0) -> dict:
    key = jax.random.key(seed)
    ks = jax.random.split(key, 22)
    nrm = lambda k, s, sc: jax.random.normal(k, s, jnp.float32) * sc
    return {
        "x_prompt": nrm(ks[0], (BATCH, SEQ, D_MODEL), 1.0),
        "x_sample": nrm(ks[1], (DEC_BATCH, DEC_SEQ, D_MODEL), 1.0),
        "state_pool": nrm(ks[2], (DEPTH, DEC_BATCH, POOL_BUF, POOL_WIDTH), 1.0),
        "state_conv": nrm(ks[3], (DEPTH, DEC_BATCH, CONV_BUF, CONV_WIDTH), 1.0),
        "w_in": nrm(ks[4], (DEPTH, D_MODEL, IN_COLS), D_MODEL ** -0.5),
        "w_pool": nrm(ks[5], (DEPTH, N_POOL_GROUPS, POOL_GROUP, POOL_GROUP), POOL_GROUP ** -0.5),
        "pool_scale": 1.0 + nrm(ks[6], (DEPTH, POOL_WIDTH), 0.1),
        "conv_w": nrm(ks[7], (DEPTH, CONV_WIDTH_K, CONV_WIDTH), CONV_WIDTH_K ** -0.5),
        "w_o": nrm(ks[8], (DEPTH, MIX_WIDTH, D_MODEL), BETA * MIX_WIDTH ** -0.5),
        "ln1_g": 1.0 + nrm(ks[9], (DEPTH, D_MODEL), 0.05),
        "ln1_b": nrm(ks[10], (DEPTH, D_MODEL), 0.02),
        "ln2_g": 1.0 + nrm(ks[11], (DEPTH, D_MODEL), 0.05),
        "ln2_b": nrm(ks[12], (DEPTH, D_MODEL), 0.02),
        "w_router": nrm(ks[13], (DEPTH, D_MODEL, N_EXPERTS), D_MODEL ** -0.5),
        "router_bias": nrm(ks[14], (DEPTH, N_EXPERTS), 0.01),
        "w_gate": nrm(ks[15], (DEPTH, N_EXPERTS, D_MODEL, D_EXPERT), D_MODEL ** -0.5),
        "w_up": nrm(ks[16], (DEPTH, N_EXPERTS, D_MODEL, D_EXPERT), D_MODEL ** -0.5),
        "w_down": nrm(ks[17], (DEPTH, N_EXPERTS, D_EXPERT, D_MODEL), BETA * D_EXPERT ** -0.5),
        "ws_gate": nrm(ks[18], (DEPTH, D_MODEL, D_EXPERT), D_MODEL ** -0.5),
        "ws_up": nrm(ks[19], (DEPTH, D_MODEL, D_EXPERT), D_MODEL ** -0.5),
        "ws_down": nrm(ks[20], (DEPTH, D_EXPERT, D_MODEL), BETA * D_EXPERT ** -0.5),
    }


def reference(x_prompt, x_sample, state_pool, state_conv, w_in, w_pool, pool_scale, conv_w, w_o,
              ln1_g, ln1_b, ln2_g, ln2_b, w_router, router_bias,
              w_gate, w_up, w_down, ws_gate, ws_up, ws_down):
    weights = (w_in, w_pool, pool_scale, conv_w, w_o, ln1_g, ln1_b, ln2_g, ln2_b,
               w_router, router_bias, w_gate, w_up, w_down, ws_gate, ws_up, ws_down)
    hp, hs = x_prompt, x_sample
    pool_p, pool_s, conv_p, conv_s = [], [], [], []
    for l in range(DEPTH):
        zp = jnp.zeros((BATCH, POOL_BUF, POOL_WIDTH), x_prompt.dtype)
        zc = jnp.zeros((BATCH, CONV_BUF, CONV_WIDTH), x_prompt.dtype)
        hp, np_, nc_ = layer(hp, zp, zc, 0, l, *weights)
        hs, ns_, ncs_ = layer(hs, state_pool[l], state_conv[l], PAST_LEN, l, *weights)
        pool_p.append(np_)
        conv_p.append(nc_)
        pool_s.append(ns_)
        conv_s.append(ncs_)
    new_pool_prompt = jnp.stack(pool_p, axis=0)
    new_pool_sample = jnp.stack(pool_s, axis=0)
    new_conv_prompt = jnp.stack(conv_p, axis=0)
    new_conv_sample = jnp.stack(conv_s, axis=0)
    return (hp, hs, new_pool_prompt, new_pool_sample, new_conv_prompt, new_conv_sample)
```

```python
import functools
import math

import jax
import jax.numpy as jnp
from jax import lax
from jax.experimental import pallas as pl
from jax.experimental.pallas import tpu as pltpu

POOL_WINDOWS = (2, 4, 8, 16)
TOP_K = 8
ROUTED_SCALE = 2.5
LN_EPS = 1e-5
PAST_LEN = 16384

LANES = 128
SUBLANES = 8
VMEM_LIMIT = 56 * 1024 * 1024

POOL_HALO = 32
CONV_HALO = 8
ROW_BLK = 128
ITEM_BLKS = 4

BF16 = jnp.bfloat16
F32 = jnp.float32


def _params(*sem):
    return pltpu.CompilerParams(dimension_semantics=sem, vmem_limit_bytes=VMEM_LIMIT)


def _pick_tile(n, target, mult):
    best = None
    for t in range(mult, min(n, target) + 1, mult):
        if n % t == 0:
            best = t
    assert best is not None, (n, target, mult)
    return best


def _dot(a, b):
    return jnp.dot(a, b, preferred_element_type=F32)


def _silu(x):
    return x * jax.nn.sigmoid(x)


def _pool_prompt_kernel(x_ref, w_ref, wp_ref, sc_ref, out_ref, st_ref, u_buf, a_buf, b_buf,
                        *, tm, windows, pool_buf):
    g = pl.program_id(0)
    t = pl.program_id(2)
    nt = pl.num_programs(2)
    halo = POOL_HALO
    rows = halo + tm

    @pl.when(t == 0)
    def _():
        u_buf[0:halo, :] = jnp.zeros((halo, u_buf.shape[1]), F32)

    u = _dot(x_ref[...], w_ref[...])
    u_buf[halo:rows, :] = u
    pos1 = (t * tm + lax.broadcasted_iota(jnp.int32, (tm, 1), 0) + 1).astype(F32)

    for gi, w in enumerate(windows):
        @pl.when(g == gi)
        def _(w=w):
            src, shift, start = u_buf, 1, SUBLANES
            dsts = (a_buf, b_buf)
            lev = 0
            while shift < w:
                dst = dsts[lev % 2]
                n = rows - start
                dst[start:rows, :] = src[start:rows, :] + src[start - shift:start - shift + n, :]
                src, shift, start, lev = dst, shift * 2, start + SUBLANES, lev + 1
            win = src[halo:rows, :]
            inv_cnt = 1.0 / jnp.minimum(pos1, float(w))
            pooled = (win * inv_cnt - u).astype(BF16)
            out_ref[...] = (_dot(pooled, wp_ref[...]) * sc_ref[...]).astype(out_ref.dtype)

    @pl.when(t == nt - 1)
    def _():
        st_ref[...] = u_buf[rows - pool_buf:rows, :]

    u_buf[0:halo, :] = u_buf[tm:rows, :]


def _pool_prompt(x16, w_in16, w_pool16, pool_scale, batch, seq, pool_buf):
    m, d = x16.shape
    ngrp, pg, _ = w_pool16.shape
    pw = ngrp * pg
    assert max(POOL_WINDOWS) <= POOL_HALO and len(POOL_WINDOWS) == ngrp and pg % LANES == 0
    tm = _pick_tile(seq, 1024, 16)
    nt = seq // tm
    kern = functools.partial(_pool_prompt_kernel, tm=tm, windows=POOL_WINDOWS, pool_buf=pool_buf)
    return pl.pallas_call(
        kern,
        grid=(ngrp, batch, nt),
        in_specs=[
            pl.BlockSpec((tm, d), lambda g, b, t: (b * nt + t, 0)),
            pl.BlockSpec((d, pg), lambda g, b, t: (0, g)),
            pl.BlockSpec((None, pg, pg), lambda g, b, t: (g, 0, 0)),
            pl.BlockSpec((1, pg), lambda g, b, t: (0, g)),
        ],
        out_specs=[
            pl.BlockSpec((tm, pg), lambda g, b, t: (b * nt + t, g)),
            pl.BlockSpec((None, pool_buf, pg), lambda g, b, t: (b, 0, g)),
        ],
        out_shape=[
            jax.ShapeDtypeStruct((m, pw), BF16),
            jax.ShapeDtypeStruct((batch, pool_buf, pw), F32),
        ],
        scratch_shapes=[pltpu.VMEM((POOL_HALO + tm, pg), F32)] * 3,
        compiler_params=_params("arbitrary", "arbitrary", "arbitrary"),
        name="pool_prompt",
    )(x16, w_in16, w_pool16, pool_scale)


def _conv_prompt_kernel(x_ref, wb_ref, wc_ref, wh_ref, cw_ref, out_ref, st_ref, v_buf,
                        *, tm, conv_buf):
    t = pl.program_id(2)
    nt = pl.num_programs(2)
    halo = CONV_HALO
    rows = halo + tm

    @pl.when(t == 0)
    def _():
        v_buf[0:halo, :] = jnp.zeros((halo, v_buf.shape[1]), F32)

    x = x_ref[...]
    v = _dot(x, wc_ref[...]) * _dot(x, wh_ref[...])
    v_buf[halo:rows, :] = v
    conv = (cw_ref[0:1, :] * v_buf[halo - 2:rows - 2, :]
            + cw_ref[1:2, :] * v_buf[halo - 1:rows - 1, :]
            + cw_ref[2:3, :] * v)
    out_ref[...] = (_dot(x, wb_ref[...]) * conv).astype(out_ref.dtype)

    @pl.when(t == nt - 1)
    def _():
        st_ref[...] = v_buf[rows - conv_buf:rows, :]

    v_buf[0:halo, :] = v_buf[tm:rows, :]


def _conv_prompt(x16, w_in16, conv_w, batch, seq, pw, cw, ct, conv_buf):
    m, d = x16.shape
    assert conv_w.shape[0] == 3 and conv_buf == 2 and cw % ct == 0 and pw % ct == 0
    nc = cw // ct
    off_b, off_c, off_h = pw // ct, (pw + cw) // ct, (pw + 2 * cw) // ct
    tm = _pick_tile(seq, 1024, 16)
    nt = seq // tm
    kern = functools.partial(_conv_prompt_kernel, tm=tm, conv_buf=conv_buf)
    return pl.pallas_call(
        kern,
        grid=(nc, batch, nt),
        in_specs=[
            pl.BlockSpec((tm, d), lambda c, b, t: (b * nt + t, 0)),
            pl.BlockSpec((d, ct), lambda c, b, t: (0, off_b + c)),
            pl.BlockSpec((d, ct), lambda c, b, t: (0, off_c + c)),
            pl.BlockSpec((d, ct), lambda c, b, t: (0, off_h + c)),
            pl.BlockSpec((3, ct), lambda c, b, t: (0, c)),
        ],
        out_specs=[
            pl.BlockSpec((tm, ct), lambda c, b, t: (b * nt + t, c)),
            pl.BlockSpec((None, conv_buf, ct), lambda c, b, t: (b, 0, c)),
        ],
        out_shape=[
            jax.ShapeDtypeStruct((m, cw), BF16),
            jax.ShapeDtypeStruct((batch, conv_buf, cw), F32),
        ],
        scratch_shapes=[pltpu.VMEM((CONV_HALO + tm, ct), F32)],
        compiler_params=_params("arbitrary", "arbitrary", "arbitrary"),
        name="conv_prompt",
    )(x16, w_in16, w_in16, w_in16, conv_w)


def _mixer_sample_kernel(x_ref, wu_ref, wb_ref, wc_ref, wh_ref, stp_ref, stc_ref, cw_ref, wp_ref,
                         sc_ref, pool_ref, conv_ref, u_ref, v_ref, *, windows, pool_buf):
    c = pl.program_id(0)
    x = x_ref[...]
    u = _dot(x, wu_ref[...])
    u_ref[...] = u
    for gi, w in enumerate(windows):
        @pl.when(c == gi)
        def _(w=w):
            win = u
            for j in range(1, w):
                win = win + stp_ref[pool_buf - j]
            cnt = float(min(PAST_LEN + 1, w))
            pooled = (win / cnt - u).astype(BF16)
            pool_ref[...] = (_dot(pooled, wp_ref[...]) * sc_ref[...]).astype(pool_ref.dtype)

    v = _dot(x, wc_ref[...]) * _dot(x, wh_ref[...])
    v_ref[...] = v
    conv = cw_ref[0:1, :] * stc_ref[0] + cw_ref[1:2, :] * stc_ref[1] + cw_ref[2:3, :] * v
    conv_ref[...] = (_dot(x, wb_ref[...]) * conv).astype(conv_ref.dtype)


def _mixer_sample(x16, w_in16, stp_t, stc_t, conv_w, w_pool16, pool_scale):
    bs, d = x16.shape
    ngrp, pg, _ = w_pool16.shape
    pool_buf, conv_buf = stp_t.shape[0], stc_t.shape[0]
    pw, cw = stp_t.shape[2], stc_t.shape[2]
    assert cw == pw, "decode mixer walks pooling and conv column tiles together"
    kern = functools.partial(_mixer_sample_kernel, windows=POOL_WINDOWS, pool_buf=pool_buf)
    col = lambda off: pl.BlockSpec((d, pg), lambda c: (0, off + c))
    tile = pl.BlockSpec((bs, pg), lambda c: (0, c))
    return pl.pallas_call(
        kern,
        grid=(ngrp,),
        in_specs=[
            pl.BlockSpec((bs, d), lambda c: (0, 0)),
            col(0), col(pw // pg), col((pw + cw) // pg), col((pw + 2 * cw) // pg),
            pl.BlockSpec((pool_buf, bs, pg), lambda c: (0, 0, c)),
            pl.BlockSpec((conv_buf, bs, pg), lambda c: (0, 0, c)),
            pl.BlockSpec((3, pg), lambda c: (0, c)),
            pl.BlockSpec((None, pg, pg), lambda c: (c, 0, 0)),
            pl.BlockSpec((1, pg), lambda c: (0, c)),
        ],
        out_specs=[tile, tile, tile, tile],
        out_shape=[
            jax.ShapeDtypeStruct((bs, pw), BF16),
            jax.ShapeDtypeStruct((bs, cw), BF16),
            jax.ShapeDtypeStruct((bs, pw), F32),
            jax.ShapeDtypeStruct((bs, cw), F32),
        ],
        compiler_params=_params("arbitrary"),
        name="mixer_sample",
    )(x16, w_in16, w_in16, w_in16, w_in16, stp_t, stc_t, conv_w, w_pool16, pool_scale)


def _layer_norm(r, g, b):
    mu = jnp.mean(r, axis=-1, keepdims=True)
    d = r - mu
    var = jnp.mean(d * d, axis=-1, keepdims=True)
    return d * lax.rsqrt(var + LN_EPS) * g + b


def _outproj_kernel(pool_ref, conv_ref, wo_ref, x_ref, g_ref, b_ref, wr_ref,
                    h32_ref, h16_ref, sc_ref, acc, *, alpha, nk_pool):
    k = pl.program_id(1)
    nk = pl.num_programs(1)

    @pl.when(k == 0)
    def _():
        acc[...] = alpha * x_ref[...]

    @pl.when(k < nk_pool)
    def _():
        acc[...] += _dot(pool_ref[...], wo_ref[...])

    @pl.when(k >= nk_pool)
    def _():
        acc[...] += _dot(conv_ref[...], wo_ref[...])

    @pl.when(k == nk - 1)
    def _():
        h = _layer_norm(acc[...], g_ref[...], b_ref[...])
        h32_ref[...] = h
        h16 = h.astype(BF16)
        h16_ref[...] = h16
        sc_ref[...] = jax.nn.sigmoid(_dot(h16, wr_ref[...]))


def _outproj(pool_all, conv_all, w_o16, x_all, ln_g, ln_b, w_r16, alpha):
    m, pw = pool_all.shape
    cw = conv_all.shape[1]
    d = w_o16.shape[1]
    e = w_r16.shape[1]
    tm = _pick_tile(m, 320, 16)
    tk = _pick_tile(math.gcd(pw, cw), 512, LANES)
    nkp, nkc = pw // tk, cw // tk
    kern = functools.partial(_outproj_kernel, alpha=alpha, nk_pool=nkp)
    row = lambda width: pl.BlockSpec((tm, width), lambda i, k: (i, 0))
    vec = pl.BlockSpec((1, d), lambda i, k: (0, 0))
    return pl.pallas_call(
        kern,
        grid=(m // tm, nkp + nkc),
        in_specs=[
            pl.BlockSpec((tm, tk), lambda i, k: (i, jnp.minimum(k, nkp - 1))),
            pl.BlockSpec((tm, tk), lambda i, k: (i, jnp.maximum(k - nkp, 0))),
            pl.BlockSpec((tk, d), lambda i, k: (k, 0)),
            row(d), vec, vec,
            pl.BlockSpec((d, e), lambda i, k: (0, 0)),
        ],
        out_specs=[row(d), row(d), row(e)],
        out_shape=[
            jax.ShapeDtypeStruct((m, d), F32),
            jax.ShapeDtypeStruct((m, d), BF16),
            jax.ShapeDtypeStruct((m, e), F32),
        ],
        scratch_shapes=[pltpu.VMEM((tm, d), F32)],
        compiler_params=_params("arbitrary", "arbitrary"),
        name="outproj_ln_router",
    )(pool_all, conv_all, w_o16, x_all, ln_g, ln_b, w_r16)


def _y_copy(ybuf, y_hbm, sems, slot, j, row0, col0, tn):
    return pltpu.make_async_copy(
        ybuf.at[slot, pl.ds(j * ROW_BLK, ROW_BLK), :],
        y_hbm.at[pl.ds(row0 + j * ROW_BLK, ROW_BLK), pl.ds(col0, tn)],
        sems.at[slot])


def _moe_kernel(e_ref, ep_ref, r0_ref, ns_ref,
                x_ref, gt_ref, wg_ref, wu_ref, wd_ref,
                y_hbm,
                gu_acc, wgu16, wd16, hb16, ybuf, sems,
                *, ka, nb, tn, de):
    del e_ref, ep_ref
    i = pl.program_id(0)
    s = pl.program_id(1)
    ni = pl.num_programs(0)
    nsub = ns_ref[i]
    last_slot = (nb - 1) % 2
    last_col = (nb - 1) * tn

    def for_live_blocks(count, fn):
        for j in range(ITEM_BLKS):
            pl.when(j < count)(functools.partial(fn, j))

    @pl.when(jnp.logical_and(s == 0, i > 0))
    def _():
        prev = jnp.maximum(i - 1, 0)
        row0 = r0_ref[prev] * ROW_BLK
        for_live_blocks(ns_ref[prev],
                        lambda j: _y_copy(ybuf, y_hbm, sems, last_slot, j, row0, last_col, tn).wait())

    @pl.when(s < ka)
    def _():
        wgu16[:, 0:de] = wg_ref[...].astype(BF16)
        wgu16[:, de:2 * de] = wu_ref[...].astype(BF16)

        @pl.when(s == 0)
        def _():
            gu_acc[...] = jnp.zeros(gu_acc.shape, F32)

        def gate_up(j):
            r = pl.ds(j * ROW_BLK, ROW_BLK)
            gu_acc[r, :] += _dot(x_ref[r, :], wgu16[...])
        for_live_blocks(nsub, gate_up)

    @pl.when(s >= ka)
    def _():
        n = s - ka
        slot = n % 2
        row0 = r0_ref[i] * ROW_BLK
        col0 = pl.multiple_of(n * tn, tn)

        @pl.when(n == 0)
        def _():
            def act(j):
                r = pl.ds(j * ROW_BLK, ROW_BLK)
                hb16[r, :] = (_silu(gu_acc[r, 0:de]) * gu_acc[r, de:2 * de]).astype(BF16)
            for_live_blocks(nsub, act)

        wd16[...] = wd_ref[...].astype(BF16)

        def down(j):
            r = pl.ds(j * ROW_BLK, ROW_BLK)
            gate = jnp.concatenate([gt_ref[r, :]] * (tn // LANES), axis=1)
            ybuf[slot, r, :] = _dot(hb16[r, :], wd16[...]) * gate
            _y_copy(ybuf, y_hbm, sems, slot, j, row0, col0, tn).start()
        for_live_blocks(nsub, down)

        @pl.when(n > 0)
        def _():
            pcol = pl.multiple_of((n - 1) * tn, tn)
            for_live_blocks(nsub, lambda j: _y_copy(ybuf, y_hbm, sems, 1 - slot, j, row0, pcol, tn).wait())

        @pl.when(jnp.logical_and(n == nb - 1, i == ni - 1))
        def _():
            for_live_blocks(nsub, lambda j: _y_copy(ybuf, y_hbm, sems, slot, j, row0, col0, tn).wait())


def _moe_routed(x_sorted, gate_rows, item_e, item_eprev, item_row0, item_nsub, w_gate, w_up, w_down, layer):
    r_alloc, d = x_sorted.shape
    de = w_gate.shape[-1]
    ni = item_e.shape[0]
    tk = _pick_tile(d, min(1024, d // 4), LANES)
    tn = tk
    ka, nb = d // tk, d // tn
    item_rows = ITEM_BLKS * ROW_BLK
    assert de % LANES == 0 and nb >= 2

    def x_map(i, s, e, ep, r0, ns):
        return (r0[i] * ROW_BLK, jnp.where(ns[i] > 0, jnp.minimum(s, ka - 1), ka - 1) * tk)

    def gt_map(i, s, e, ep, r0, ns):
        return (r0[i] * ROW_BLK, 0)

    def wgu_map(i, s, e, ep, r0, ns):
        return (layer, e[i], jnp.where(ns[i] > 0, jnp.minimum(s, ka - 1), ka - 1), 0)

    def wd_map(i, s, e, ep, r0, ns):
        live = ns[i] > 0
        first = s < ka
        ex = jnp.where(first, ep[i], e[i])
        blk = jnp.where(jnp.logical_and(live, jnp.logical_not(first)), s - ka, nb - 1)
        return (layer, ex, 0, blk)

    kern = functools.partial(_moe_kernel, ka=ka, nb=nb, tn=tn, de=de)
    grid_spec = pltpu.PrefetchScalarGridSpec(
        num_scalar_prefetch=4,
        grid=(ni, ka + nb),
        in_specs=[
            pl.BlockSpec((pl.Element(item_rows), pl.Element(tk)), x_map),
            pl.BlockSpec((pl.Element(item_rows), pl.Element(LANES)), gt_map),
            pl.BlockSpec((None, None, tk, de), wgu_map),
            pl.BlockSpec((None, None, tk, de), wgu_map),
            pl.BlockSpec((None, None, de, tn), wd_map),
        ],
        out_specs=pl.BlockSpec(memory_space=pl.ANY),
        scratch_shapes=[
            pltpu.VMEM((item_rows, 2 * de), F32),
            pltpu.VMEM((tk, 2 * de), BF16),
            pltpu.VMEM((de, tn), BF16),
            pltpu.VMEM((item_rows, de), BF16),
            pltpu.VMEM((2, item_rows, tn), F32),
            pltpu.SemaphoreType.DMA((2,)),
        ],
    )
    return pl.pallas_call(
        kern,
        grid_spec=grid_spec,
        out_shape=jax.ShapeDtypeStruct((r_alloc, d), F32),
        compiler_params=_params("arbitrary", "arbitrary"),
        name="moe_routed",
    )(item_e, item_eprev, item_row0, item_nsub, x_sorted, gate_rows, w_gate, w_up, w_down)


def _route(scores, bias, n_experts):
    m = scores.shape[0]
    a = m * TOP_K
    _, idx = lax.top_k(scores + bias[None, :].astype(F32), TOP_K)
    sel = jnp.take_along_axis(scores, idx, axis=1)
    gates = sel / jnp.sum(sel, axis=-1, keepdims=True) * ROUTED_SCALE

    e_flat = idx.reshape(-1).astype(jnp.int32)
    order = jnp.argsort(e_flat, stable=True).astype(jnp.int32)
    e_s = e_flat[order]
    counts = jnp.bincount(e_flat, length=n_experts).astype(jnp.int32)
    padded = (counts + ROW_BLK - 1) // ROW_BLK * ROW_BLK
    starts = jnp.cumsum(counts) - counts
    pstarts = jnp.cumsum(padded) - padded
    dest = pstarts[e_s] + (jnp.arange(a, dtype=jnp.int32) - starts[e_s])

    item_rows = ITEM_BLKS * ROW_BLK
    r_max = (a + n_experts * (ROW_BLK - 1)) // ROW_BLK * ROW_BLK
    r_alloc = r_max + item_rows
    row_tok = jnp.zeros((r_alloc,), jnp.int32).at[dest].set(order // TOP_K)
    row_gate = jnp.zeros((r_alloc,), F32).at[dest].set(gates.reshape(-1)[order])
    pos = jnp.zeros((a,), jnp.int32).at[order].set(dest).reshape(m, TOP_K)

    ni = n_experts + a // item_rows
    n_items = (padded + item_rows - 1) // item_rows
    item_end = jnp.cumsum(n_items)
    total = item_end[-1]
    slots = jnp.arange(ni, dtype=jnp.int32)
    live = slots < total
    sl = jnp.minimum(slots, total - 1)
    ex = jnp.searchsorted(item_end, sl, side="right").astype(jnp.int32)
    jj = sl - (item_end[ex] - n_items[ex])
    row0 = pstarts[ex] // ROW_BLK + jj * ITEM_BLKS
    nsub = jnp.where(live, jnp.minimum(ITEM_BLKS, (padded[ex] - jj * item_rows) // ROW_BLK), 0)
    ex_prev = jnp.concatenate([ex[:1], ex[:-1]])
    return (row_tok, row_gate, pos, ex.astype(jnp.int32), ex_prev.astype(jnp.int32),
            row0.astype(jnp.int32), nsub.astype(jnp.int32))


def _shared_kernel(h_ref, wgu_ref, wd_ref, out_ref, *, ch):
    c = pl.program_id(1)
    gu = _dot(h_ref[...], wgu_ref[...])
    hb = (_silu(gu[:, 0:ch]) * gu[:, ch:2 * ch]).astype(BF16)
    part = _dot(hb, wd_ref[...])

    @pl.when(c == 0)
    def _():
        out_ref[...] = part

    @pl.when(c > 0)
    def _():
        out_ref[...] += part


def _shared_ffn(h16, ws_gate16, ws_up16, ws_down16):
    m, d = h16.shape
    de = ws_gate16.shape[1]
    ch = _pick_tile(de, 384, LANES)
    nc = de // ch
    tm = _pick_tile(m, 320, 16)
    wgu = jnp.concatenate(
        [jnp.concatenate([ws_gate16[:, c * ch:(c + 1) * ch], ws_up16[:, c * ch:(c + 1) * ch]], axis=1)
         for c in range(nc)], axis=1)
    return pl.pallas_call(
        functools.partial(_shared_kernel, ch=ch),
        grid=(m // tm, nc),
        in_specs=[
            pl.BlockSpec((tm, d), lambda i, c: (i, 0)),
            pl.BlockSpec((d, 2 * ch), lambda i, c: (0, c)),
            pl.BlockSpec((ch, d), lambda i, c: (c, 0)),
        ],
        out_specs=pl.BlockSpec((tm, d), lambda i, c: (i, 0)),
        out_shape=jax.ShapeDtypeStruct((m, d), F32),
        compiler_params=_params("arbitrary", "arbitrary"),
        name="shared_ffn",
    )(h16, wgu, ws_down16)


def _final_ln_kernel(h_ref, r_ref, s_ref, g_ref, b_ref, out_ref, *, alpha):
    out_ref[...] = _layer_norm(alpha * h_ref[...] + (r_ref[...] + s_ref[...]), g_ref[...], b_ref[...])


def _final_ln(h32, routed, shared, ln_g, ln_b, alpha):
    m, d = h32.shape
    tm = _pick_tile(m, 128, SUBLANES)
    row = pl.BlockSpec((tm, d), lambda i: (i, 0))
    vec = pl.BlockSpec((1, d), lambda i: (0, 0))
    return pl.pallas_call(
        functools.partial(_final_ln_kernel, alpha=alpha),
        grid=(m // tm,),
        in_specs=[row, row, row, vec, vec],
        out_specs=row,
        out_shape=jax.ShapeDtypeStruct((m, d), F32),
        compiler_params=_params("arbitrary"),
        name="final_ln",
    )(h32, routed, shared, ln_g, ln_b)


def _layer(xp, xs, st_pool, st_conv, alpha, layer, w_in, w_pool, pool_scale, conv_w, w_o,
           ln1_g, ln1_b, ln2_g, ln2_b, w_router, router_bias,
           w_gate, w_up, w_down, ws_gate, ws_up, ws_down):
    batch, seq, d = xp.shape
    bs = xs.shape[0]
    pool_buf, pw = st_pool.shape[1], st_pool.shape[2]
    conv_buf, cw = st_conv.shape[1], st_conv.shape[2]
    n_experts = w_router.shape[1]
    pg = w_pool.shape[1]

    xp2 = xp.reshape(batch * seq, d)
    xs2 = xs.reshape(bs, d)
    w_in16 = w_in.astype(BF16)
    w_pool16 = w_pool.astype(BF16)
    scale2 = pool_scale.reshape(1, pw)

    xp16 = xp2.astype(BF16)
    pool_p, new_pool_p = _pool_prompt(xp16, w_in16, w_pool16, scale2, batch, seq, pool_buf)
    conv_p, new_conv_p = _conv_prompt(xp16, w_in16, conv_w, batch, seq, pw, cw, pg, conv_buf)

    stp_t = jnp.transpose(st_pool, (1, 0, 2))
    stc_t = jnp.transpose(st_conv, (1, 0, 2))
    pool_s, conv_s, u_s, v_s = _mixer_sample(xs2.astype(BF16), w_in16, stp_t, stc_t, conv_w,
                                             w_pool16, scale2)
    new_pool_s = jnp.concatenate([st_pool[:, 1:], u_s[:, None, :]], axis=1)
    new_conv_s = jnp.concatenate([st_conv[:, 1:], v_s[:, None, :]], axis=1)

    x_all = jnp.concatenate([xp2, xs2], axis=0)
    pool_all = jnp.concatenate([pool_p, pool_s], axis=0)
    conv_all = jnp.concatenate([conv_p, conv_s], axis=0)
    h32, h16, scores = _outproj(pool_all, conv_all, w_o.astype(BF16), x_all,
                                ln1_g.reshape(1, d), ln1_b.reshape(1, d),
                                w_router.astype(BF16), alpha)

    row_tok, row_gate, pos, item_e, item_ep, item_row0, item_nsub = _route(scores, router_bias, n_experts)
    x_sorted = jnp.take(h16, row_tok, axis=0)
    gate_rows = jnp.broadcast_to(row_gate[:, None], (row_gate.shape[0], LANES))
    y_sorted = _moe_routed(x_sorted, gate_rows, item_e, item_ep, item_row0, item_nsub,
                           w_gate, w_up, w_down, layer)
    routed = jnp.sum(jnp.take(y_sorted, pos, axis=0), axis=1)

    shared = _shared_ffn(h16, ws_gate.astype(BF16), ws_up.astype(BF16), ws_down.astype(BF16))
    y = _final_ln(h32, routed, shared, ln2_g.reshape(1, d), ln2_b.reshape(1, d), alpha)
    yp = y[:batch * seq].reshape(batch, seq, d)
    ys = y[batch * seq:].reshape(bs, xs.shape[1], d)
    return yp, ys, new_pool_p, new_pool_s, new_conv_p, new_conv_s


def kernel(x_prompt, x_sample, state_pool, state_conv, w_in, w_pool, pool_scale, conv_w, w_o,
           ln1_g, ln1_b, ln2_g, ln2_b, w_router, router_bias,
           w_gate, w_up, w_down, ws_gate, ws_up, ws_down):
    depth = w_in.shape[0]
    assert x_sample.shape[1] == 1, "decode rows carry one new token per sequence"
    alpha = (2.0 * depth) ** 0.25
    hp, hs = x_prompt, x_sample
    pool_p, pool_s, conv_p, conv_s = [], [], [], []
    for l in range(depth):
        hp, hs, npp, nps, ncp, ncs = _layer(
            hp, hs, state_pool[l], state_conv[l], alpha, l, w_in[l], w_pool[l], pool_scale[l],
            conv_w[l], w_o[l], ln1_g[l], ln1_b[l], ln2_g[l], ln2_b[l], w_router[l],
            router_bias[l], w_gate, w_up, w_down, ws_gate[l], ws_up[l], ws_down[l])
        pool_p.append(npp)
        pool_s.append(nps)
        conv_p.append(ncp)
        conv_s.append(ncs)
    return (hp, hs, jnp.stack(pool_p, 0), jnp.stack(pool_s, 0), jnp.stack(conv_p, 0), jnp.stack(conv_s, 0))
```

```python
import functools
import math

import jax
import jax.numpy as jnp
from jax import lax
from jax.experimental import pallas as pl
from jax.experimental.pallas import tpu as pltpu

POOL_WINDOWS = (2, 4, 8, 16)
TOP_K = 8
ROUTED_SCALE = 2.5
LN_EPS = 1e-5
PAST_LEN = 16384

LANES = 128
SUBLANES = 8
VMEM_LIMIT = 56 * 1024 * 1024

POOL_HALO = 32
CONV_HALO = 8
ITEM_ROWS = 512
ROW_GRAIN = 64
TOP_K_SHIFT = 3
assert 1 << TOP_K_SHIFT == TOP_K and ITEM_ROWS % ROW_GRAIN == 0

BF16 = jnp.bfloat16
F32 = jnp.float32


def _params(*sem):
    return pltpu.CompilerParams(dimension_semantics=sem, vmem_limit_bytes=VMEM_LIMIT)


def _pick_tile(n, target, mult):
    best = None
    for t in range(mult, min(n, target) + 1, mult):
        if n % t == 0:
            best = t
    assert best is not None, (n, target, mult)
    return best


def _dot(a, b):
    return jnp.dot(a, b, preferred_element_type=F32)


def _silu(x):
    return x * jax.nn.sigmoid(x)


def _pack_halves(x16):
    half = x16.shape[1] // 2
    bits = lax.bitcast_convert_type(x16.astype(F32), jnp.uint32)
    return (bits[:, :half] >> 16) | bits[:, half:]


def _unpack_half(words, high):
    bits = jnp.where(high, words & jnp.uint32(0xFFFF0000), words << 16)
    return lax.bitcast_convert_type(bits, F32).astype(BF16)


def _pool_prompt_kernel(x_ref, w_ref, wp_ref, sc_ref, out_ref, st_ref, u_buf, a_buf, b_buf,
                        *, tm, windows, pool_buf):
    g = pl.program_id(0)
    t = pl.program_id(2)
    nt = pl.num_programs(2)
    halo = POOL_HALO
    rows = halo + tm

    @pl.when(t == 0)
    def _():
        u_buf[0:halo, :] = jnp.zeros((halo, u_buf.shape[1]), F32)

    u = _dot(x_ref[...], w_ref[...])
    u_buf[halo:rows, :] = u
    pos1 = (t * tm + lax.broadcasted_iota(jnp.int32, (tm, 1), 0) + 1).astype(F32)

    for gi, w in enumerate(windows):
        @pl.when(g == gi)
        def _(w=w):
            src, shift, start = u_buf, 1, SUBLANES
            dsts = (a_buf, b_buf)
            lev = 0
            while shift < w:
                dst = dsts[lev % 2]
                n = rows - start
                dst[start:rows, :] = src[start:rows, :] + src[start - shift:start - shift + n, :]
                src, shift, start, lev = dst, shift * 2, start + SUBLANES, lev + 1
            win = src[halo:rows, :]
            inv_cnt = 1.0 / jnp.minimum(pos1, float(w))
            pooled = (win * inv_cnt - u).astype(BF16)
            out_ref[...] = (_dot(pooled, wp_ref[...]) * sc_ref[...]).astype(out_ref.dtype)

    @pl.when(t == nt - 1)
    def _():
        st_ref[...] = u_buf[rows - pool_buf:rows, :]

    u_buf[0:halo, :] = u_buf[tm:rows, :]


def _pool_prompt(x16, w_in16, w_pool16, pool_scale, batch, seq, pool_buf):
    m, d = x16.shape
    ngrp, pg, _ = w_pool16.shape
    pw = ngrp * pg
    assert max(POOL_WINDOWS) <= POOL_HALO and len(POOL_WINDOWS) == ngrp and pg % LANES == 0
    tm = _pick_tile(seq, 1024, 16)
    nt = seq // tm
    kern = functools.partial(_pool_prompt_kernel, tm=tm, windows=POOL_WINDOWS, pool_buf=pool_buf)
    return pl.pallas_call(
        kern,
        grid=(ngrp, batch, nt),
        in_specs=[
            pl.BlockSpec((tm, d), lambda g, b, t: (b * nt + t, 0)),
            pl.BlockSpec((d, pg), lambda g, b, t: (0, g)),
            pl.BlockSpec((None, pg, pg), lambda g, b, t: (g, 0, 0)),
            pl.BlockSpec((1, pg), lambda g, b, t: (0, g)),
        ],
        out_specs=[
            pl.BlockSpec((tm, pg), lambda g, b, t: (b * nt + t, g)),
            pl.BlockSpec((None, pool_buf, pg), lambda g, b, t: (b, 0, g)),
        ],
        out_shape=[
            jax.ShapeDtypeStruct((m, pw), BF16),
            jax.ShapeDtypeStruct((batch, pool_buf, pw), F32),
        ],
        scratch_shapes=[pltpu.VMEM((POOL_HALO + tm, pg), F32)] * 3,
        compiler_params=_params("arbitrary", "arbitrary", "arbitrary"),
        name="pool_prompt",
    )(x16, w_in16, w_pool16, pool_scale)


def _conv_prompt_kernel(x_ref, wb_ref, wc_ref, wh_ref, cw_ref, out_ref, st_ref, v_buf,
                        *, tm, conv_buf):
    t = pl.program_id(2)
    nt = pl.num_programs(2)
    halo = CONV_HALO
    rows = halo + tm

    @pl.when(t == 0)
    def _():
        v_buf[0:halo, :] = jnp.zeros((halo, v_buf.shape[1]), F32)

    x = x_ref[...]
    v = _dot(x, wc_ref[...]) * _dot(x, wh_ref[...])
    v_buf[halo:rows, :] = v
    conv = (cw_ref[0:1, :] * v_buf[halo - 2:rows - 2, :]
            + cw_ref[1:2, :] * v_buf[halo - 1:rows - 1, :]
            + cw_ref[2:3, :] * v)
    out_ref[...] = (_dot(x, wb_ref[...]) * conv).astype(out_ref.dtype)

    @pl.when(t == nt - 1)
    def _():
        st_ref[...] = v_buf[rows - conv_buf:rows, :]

    v_buf[0:halo, :] = v_buf[tm:rows, :]


def _conv_prompt(x16, w_in16, conv_w, batch, seq, pw, cw, ct, conv_buf):
    m, d = x16.shape
    assert conv_w.shape[0] == 3 and conv_buf == 2 and cw % ct == 0 and pw % ct == 0
    nc = cw // ct
    off_b, off_c, off_h = pw // ct, (pw + cw) // ct, (pw + 2 * cw) // ct
    tm = _pick_tile(seq, 1024, 16)
    nt = seq // tm
    kern = functools.partial(_conv_prompt_kernel, tm=tm, conv_buf=conv_buf)
    return pl.pallas_call(
        kern,
        grid=(nc, batch, nt),
        in_specs=[
            pl.BlockSpec((tm, d), lambda c, b, t: (b * nt + t, 0)),
            pl.BlockSpec((d, ct), lambda c, b, t: (0, off_b + c)),
            pl.BlockSpec((d, ct), lambda c, b, t: (0, off_c + c)),
            pl.BlockSpec((d, ct), lambda c, b, t: (0, off_h + c)),
            pl.BlockSpec((3, ct), lambda c, b, t: (0, c)),
        ],
        out_specs=[
            pl.BlockSpec((tm, ct), lambda c, b, t: (b * nt + t, c)),
            pl.BlockSpec((None, conv_buf, ct), lambda c, b, t: (b, 0, c)),
        ],
        out_shape=[
            jax.ShapeDtypeStruct((m, cw), BF16),
            jax.ShapeDtypeStruct((batch, conv_buf, cw), F32),
        ],
        scratch_shapes=[pltpu.VMEM((CONV_HALO + tm, ct), F32)],
        compiler_params=_params("arbitrary", "arbitrary", "arbitrary"),
        name="conv_prompt",
    )(x16, w_in16, w_in16, w_in16, conv_w)


def _mixer_sample_kernel(x_ref, wu_ref, wb_ref, wc_ref, wh_ref, stp_ref, stc_ref, cw_ref, wp_ref,
                         sc_ref, pool_ref, conv_ref, u_ref, v_ref, *, windows, pool_buf):
    c = pl.program_id(0)
    x = x_ref[...]
    u = _dot(x, wu_ref[...])
    u_ref[...] = u
    for gi, w in enumerate(windows):
        @pl.when(c == gi)
        def _(w=w):
            win = u
            for j in range(1, w):
                win = win + stp_ref[pool_buf - j]
            cnt = float(min(PAST_LEN + 1, w))
            pooled = (win / cnt - u).astype(BF16)
            pool_ref[...] = (_dot(pooled, wp_ref[...]) * sc_ref[...]).astype(pool_ref.dtype)

    v = _dot(x, wc_ref[...]) * _dot(x, wh_ref[...])
    v_ref[...] = v
    conv = cw_ref[0:1, :] * stc_ref[0] + cw_ref[1:2, :] * stc_ref[1] + cw_ref[2:3, :] * v
    conv_ref[...] = (_dot(x, wb_ref[...]) * conv).astype(conv_ref.dtype)


def _mixer_sample(x16, w_in16, stp_t, stc_t, conv_w, w_pool16, pool_scale):
    bs, d = x16.shape
    ngrp, pg, _ = w_pool16.shape
    pool_buf, conv_buf = stp_t.shape[0], stc_t.shape[0]
    pw, cw = stp_t.shape[2], stc_t.shape[2]
    assert cw == pw, "decode mixer walks pooling and conv column tiles together"
    kern = functools.partial(_mixer_sample_kernel, windows=POOL_WINDOWS, pool_buf=pool_buf)
    col = lambda off: pl.BlockSpec((d, pg), lambda c: (0, off + c))
    tile = pl.BlockSpec((bs, pg), lambda c: (0, c))
    return pl.pallas_call(
        kern,
        grid=(ngrp,),
        in_specs=[
            pl.BlockSpec((bs, d), lambda c: (0, 0)),
            col(0), col(pw // pg), col((pw + cw) // pg), col((pw + 2 * cw) // pg),
            pl.BlockSpec((pool_buf, bs, pg), lambda c: (0, 0, c)),
            pl.BlockSpec((conv_buf, bs, pg), lambda c: (0, 0, c)),
            pl.BlockSpec((3, pg), lambda c: (0, c)),
            pl.BlockSpec((None, pg, pg), lambda c: (c, 0, 0)),
            pl.BlockSpec((1, pg), lambda c: (0, c)),
        ],
        out_specs=[tile, tile, tile, tile],
        out_shape=[
            jax.ShapeDtypeStruct((bs, pw), BF16),
            jax.ShapeDtypeStruct((bs, cw), BF16),
            jax.ShapeDtypeStruct((bs, pw), F32),
            jax.ShapeDtypeStruct((bs, cw), F32),
        ],
        compiler_params=_params("arbitrary"),
        name="mixer_sample",
    )(x16, w_in16, w_in16, w_in16, w_in16, stp_t, stc_t, conv_w, w_pool16, pool_scale)


def _layer_norm(r, g, b):
    mu = jnp.mean(r, axis=-1, keepdims=True)
    d = r - mu
    var = jnp.mean(d * d, axis=-1, keepdims=True)
    return d * lax.rsqrt(var + LN_EPS) * g + b


def _outproj_kernel(pool_ref, conv_ref, wo_ref, x_ref, g_ref, b_ref, wr_ref,
                    h32_ref, h16_ref, hp_ref, sc_ref, acc, *, alpha, nk_pool):
    k = pl.program_id(1)
    nk = pl.num_programs(1)

    @pl.when(k == 0)
    def _():
        acc[...] = alpha * x_ref[...]

    @pl.when(k < nk_pool)
    def _():
        acc[...] += _dot(pool_ref[...], wo_ref[...])

    @pl.when(k >= nk_pool)
    def _():
        acc[...] += _dot(conv_ref[...], wo_ref[...])

    @pl.when(k == nk - 1)
    def _():
        h = _layer_norm(acc[...], g_ref[...], b_ref[...])
        h32_ref[...] = h
        h16 = h.astype(BF16)
        h16_ref[...] = h16
        hp_ref[...] = _pack_halves(h16)
        sc_ref[...] = jax.nn.sigmoid(_dot(h16, wr_ref[...]))


def _outproj(pool_all, conv_all, w_o16, x_all, ln_g, ln_b, w_r16, alpha):
    m, pw = pool_all.shape
    cw = conv_all.shape[1]
    d = w_o16.shape[1]
    e = w_r16.shape[1]
    tm = _pick_tile(m, 320, 16)
    tk = _pick_tile(math.gcd(pw, cw), 512, LANES)
    nkp, nkc = pw // tk, cw // tk
    kern = functools.partial(_outproj_kernel, alpha=alpha, nk_pool=nkp)
    row = lambda width: pl.BlockSpec((tm, width), lambda i, k: (i, 0))
    vec = pl.BlockSpec((1, d), lambda i, k: (0, 0))
    return pl.pallas_call(
        kern,
        grid=(m // tm, nkp + nkc),
        in_specs=[
            pl.BlockSpec((tm, tk), lambda i, k: (i, jnp.minimum(k, nkp - 1))),
            pl.BlockSpec((tm, tk), lambda i, k: (i, jnp.maximum(k - nkp, 0))),
            pl.BlockSpec((tk, d), lambda i, k: (k, 0)),
            row(d), vec, vec,
            pl.BlockSpec((d, e), lambda i, k: (0, 0)),
        ],
        out_specs=[row(d), row(d), row(d // 2), row(e)],
        out_shape=[
            jax.ShapeDtypeStruct((m, d), F32),
            jax.ShapeDtypeStruct((m, d), BF16),
            jax.ShapeDtypeStruct((m, d // 2), jnp.uint32),
            jax.ShapeDtypeStruct((m, e), F32),
        ],
        scratch_shapes=[pltpu.VMEM((tm, d), F32)],
        compiler_params=_params("arbitrary", "arbitrary"),
        name="outproj_ln_router",
    )(pool_all, conv_all, w_o16, x_all, ln_g, ln_b, w_r16)


def _slot_row(a, m_tok):
    return (a & (TOP_K - 1)) * m_tok + _token_of(a)


def _token_of(a):
    return lax.shift_right_logical(a, TOP_K_SHIFT)


def _moe_kernel(order_ref, e_ref, ep_ref, st_ref, cnt_ref,
                hp_hbm, wg_ref, wu_ref, wd_ref,
                y_hbm,
                xbuf, gu_acc, hb16, ybuf, gsem, ssem,
                *, ka, nb, tk, tn, de, m_tok):
    del e_ref, ep_ref
    i = pl.program_id(0)
    s = pl.program_id(1)
    ni = pl.num_programs(0)
    cnt = cnt_ref[i]
    slot = i % 2
    variant = (cnt + ROW_GRAIN - 1) // ROW_GRAIN

    def gather_copy(item_slot, j, tok):
        return pltpu.make_async_copy(hp_hbm.at[pl.ds(tok, 1), :], xbuf.at[item_slot, pl.ds(j, 1), :],
                                     gsem.at[item_slot])

    def scatter_copy(j, row):
        return pltpu.make_async_copy(ybuf.at[pl.ds(j, 1), :], y_hbm.at[pl.ds(row, 1), :], ssem.at[0])

    def start_gather(item, item_slot):
        base = st_ref[item]

        def body(j, carry):
            gather_copy(item_slot, j, _token_of(order_ref[base + j])).start()
            return carry
        lax.fori_loop(0, cnt_ref[item], body, 0)

    def wait_rows(count, block_copy):
        p = ITEM_ROWS
        while p >= 1:
            pl.when((count & p) != 0)(functools.partial(lambda q: block_copy(q).wait(), p))
            p //= 2

    def wait_gather(item, item_slot):
        wait_rows(cnt_ref[item], lambda p: pltpu.make_async_copy(
            hp_hbm.at[pl.ds(0, p), :], xbuf.at[item_slot, pl.ds(0, p), :], gsem.at[item_slot]))

    def wait_scatter(item):
        wait_rows(cnt_ref[item], lambda p: pltpu.make_async_copy(
            ybuf.at[pl.ds(0, p), :], y_hbm.at[pl.ds(0, p), :], ssem.at[0]))

    def for_variant(fn):
        for v in range(1, ITEM_ROWS // ROW_GRAIN + 1):
            pl.when(variant == v)(functools.partial(fn, v * ROW_GRAIN))

    @pl.when(s == 0)
    def _():
        @pl.when(i == 0)
        def _():
            xbuf[...] = jnp.zeros(xbuf.shape, xbuf.dtype)
            start_gather(0, 0)

        wait_gather(i, slot)

        @pl.when(i + 1 < ni)
        def _():
            start_gather(i + 1, 1 - slot)

    @pl.when(s < ka)
    def _():
        chunks_per_half = ka // 2
        high = s >= chunks_per_half
        coff = pl.multiple_of((s % chunks_per_half) * tk, tk)

        def gate_up(rows):
            x16 = _unpack_half(xbuf[slot, 0:rows, pl.ds(coff, tk)], high)
            g = _dot(x16, wg_ref[...].astype(BF16))
            u = _dot(x16, wu_ref[...].astype(BF16))

            @pl.when(s == 0)
            def _():
                gu_acc[0:rows, 0:de] = g
                gu_acc[0:rows, de:2 * de] = u

            @pl.when(s > 0)
            def _():
                gu_acc[0:rows, 0:de] += g
                gu_acc[0:rows, de:2 * de] += u
        for_variant(gate_up)

    @pl.when(s >= ka)
    def _():
        n = s - ka
        col0 = pl.multiple_of(n * tn, tn)

        @pl.when(n == 0)
        def _():
            @pl.when(i > 0)
            def _():
                wait_scatter(jnp.maximum(i - 1, 0))

            def act(rows):
                hb16[0:rows, :] = (_silu(gu_acc[0:rows, 0:de]) * gu_acc[0:rows, de:2 * de]).astype(BF16)
            for_variant(act)

        def down(rows):
            ybuf[0:rows, pl.ds(col0, tn)] = _dot(hb16[0:rows, :], wd_ref[...].astype(BF16))
        for_variant(down)

        @pl.when(n == nb - 1)
        def _():
            base = st_ref[i]

            def body(j, carry):
                scatter_copy(j, _slot_row(order_ref[base + j], m_tok)).start()
                return carry
            lax.fori_loop(0, cnt, body, 0)

            @pl.when(i == ni - 1)
            def _():
                wait_scatter(i)


def _moe_routed(hp, order, item_e, item_eprev, item_start, item_cnt, w_gate, w_up, w_down, layer):
    m_tok, dh = hp.shape
    d = 2 * dh
    de = w_gate.shape[-1]
    ni = item_e.shape[0]
    tk = _pick_tile(dh, min(1024, d // 4), LANES)
    tn = tk
    ka, nb = d // tk, d // tn
    assert de % LANES == 0 and nb >= 2 and ka % 2 == 0

    def wgu_map(i, s, order, e, ep, st, cnt):
        return (layer, e[i], jnp.where(cnt[i] > 0, jnp.minimum(s, ka - 1), ka - 1), 0)

    def wd_map(i, s, order, e, ep, st, cnt):
        live = cnt[i] > 0
        first = s < ka
        ex = jnp.where(first, ep[i], e[i])
        blk = jnp.where(jnp.logical_and(live, jnp.logical_not(first)), s - ka, nb - 1)
        return (layer, ex, 0, blk)

    kern = functools.partial(_moe_kernel, ka=ka, nb=nb, tk=tk, tn=tn, de=de, m_tok=m_tok)
    grid_spec = pltpu.PrefetchScalarGridSpec(
        num_scalar_prefetch=5,
        grid=(ni, ka + nb),
        in_specs=[
            pl.BlockSpec(memory_space=pl.ANY),
            pl.BlockSpec((None, None, tk, de), wgu_map),
            pl.BlockSpec((None, None, tk, de), wgu_map),
            pl.BlockSpec((None, None, de, tn), wd_map),
        ],
        out_specs=pl.BlockSpec(memory_space=pl.ANY),
        scratch_shapes=[
            pltpu.VMEM((2, ITEM_ROWS, dh), jnp.uint32),
            pltpu.VMEM((ITEM_ROWS, 2 * de), F32),
            pltpu.VMEM((ITEM_ROWS, de), BF16),
            pltpu.VMEM((ITEM_ROWS, d), F32),
            pltpu.SemaphoreType.DMA((2,)),
            pltpu.SemaphoreType.DMA((1,)),
        ],
    )
    return pl.pallas_call(
        kern,
        grid_spec=grid_spec,
        out_shape=jax.ShapeDtypeStruct((TOP_K * m_tok, d), F32),
        compiler_params=_params("arbitrary", "arbitrary"),
        name="moe_routed",
    )(order, item_e, item_eprev, item_start, item_cnt, hp, w_gate, w_up, w_down)


def _route(scores, bias, n_experts):
    m = scores.shape[0]
    a = m * TOP_K
    _, idx = lax.top_k(scores + bias[None, :].astype(F32), TOP_K)
    sel = jnp.take_along_axis(scores, idx, axis=1)
    gates = sel / jnp.sum(sel, axis=-1, keepdims=True) * ROUTED_SCALE

    abits = max(1, (a - 1).bit_length())
    assert (n_experts << abits) < 2 ** 31
    e_flat = idx.reshape(-1).astype(jnp.int32)
    keys = lax.sort((e_flat << abits) | jnp.arange(a, dtype=jnp.int32))
    order = keys & ((1 << abits) - 1)
    bounds = jnp.searchsorted(keys >> abits, jnp.arange(n_experts + 1, dtype=jnp.int32), side="left")
    starts = bounds[:-1].astype(jnp.int32)
    counts = (bounds[1:] - bounds[:-1]).astype(jnp.int32)

    ni = n_experts + a // ITEM_ROWS
    n_items = (counts + ITEM_ROWS - 1) // ITEM_ROWS
    item_end = jnp.cumsum(n_items)
    total = item_end[-1]
    slots = jnp.arange(ni, dtype=jnp.int32)
    live = slots < total
    sl = jnp.minimum(slots, total - 1)
    ex = jnp.searchsorted(item_end, sl, side="right").astype(jnp.int32)
    jj = sl - (item_end[ex] - n_items[ex])
    item_start = starts[ex] + jj * ITEM_ROWS
    item_cnt = jnp.where(live, jnp.minimum(ITEM_ROWS, counts[ex] - jj * ITEM_ROWS), 0)
    ex_prev = jnp.concatenate([ex[:1], ex[:-1]])
    return (gates, order.astype(jnp.int32), ex, ex_prev.astype(jnp.int32),
            item_start.astype(jnp.int32), item_cnt.astype(jnp.int32))


def _shared_kernel(h_ref, wgu_ref, wd_ref, out_ref, *, ch):
    c = pl.program_id(1)
    gu = _dot(h_ref[...], wgu_ref[...])
    hb = (_silu(gu[:, 0:ch]) * gu[:, ch:2 * ch]).astype(BF16)
    part = _dot(hb, wd_ref[...])

    @pl.when(c == 0)
    def _():
        out_ref[...] = part

    @pl.when(c > 0)
    def _():
        out_ref[...] += part


def _shared_ffn(h16, ws_gate16, ws_up16, ws_down16):
    m, d = h16.shape
    de = ws_gate16.shape[1]
    ch = _pick_tile(de, 384, LANES)
    nc = de // ch
    tm = _pick_tile(m, 320, 16)
    wgu = jnp.concatenate(
        [jnp.concatenate([ws_gate16[:, c * ch:(c + 1) * ch], ws_up16[:, c * ch:(c + 1) * ch]], axis=1)
         for c in range(nc)], axis=1)
    return pl.pallas_call(
        functools.partial(_shared_kernel, ch=ch),
        grid=(m // tm, nc),
        in_specs=[
            pl.BlockSpec((tm, d), lambda i, c: (i, 0)),
            pl.BlockSpec((d, 2 * ch), lambda i, c: (0, c)),
            pl.BlockSpec((ch, d), lambda i, c: (c, 0)),
        ],
        out_specs=pl.BlockSpec((tm, d), lambda i, c: (i, 0)),
        out_shape=jax.ShapeDtypeStruct((m, d), F32),
        compiler_params=_params("arbitrary", "arbitrary"),
        name="shared_ffn",
    )(h16, wgu, ws_down16)


def _combine_ln_kernel(h_ref, y_ref, gt_ref, s_ref, g_ref, b_ref, out_ref, *, alpha):
    routed = gt_ref[:, 0:1] * y_ref[0]
    for k in range(1, TOP_K):
        routed = routed + gt_ref[:, k:k + 1] * y_ref[k]
    out_ref[...] = _layer_norm(alpha * h_ref[...] + (routed + s_ref[...]), g_ref[...], b_ref[...])


def _combine_ln(h32, y_slots, gates, shared, ln_g, ln_b, alpha):
    m, d = h32.shape
    tm = _pick_tile(m, 64, SUBLANES)
    row = pl.BlockSpec((tm, d), lambda i: (i, 0))
    vec = pl.BlockSpec((1, d), lambda i: (0, 0))
    return pl.pallas_call(
        functools.partial(_combine_ln_kernel, alpha=alpha),
        grid=(m // tm,),
        in_specs=[row, pl.BlockSpec((TOP_K, tm, d), lambda i: (0, i, 0)),
                  pl.BlockSpec((tm, TOP_K), lambda i: (i, 0)), row, vec, vec],
        out_specs=row,
        out_shape=jax.ShapeDtypeStruct((m, d), F32),
        compiler_params=_params("arbitrary"),
        name="combine_ln",
    )(h32, y_slots, gates, shared, ln_g, ln_b)


def _layer(xp, xs, st_pool, st_conv, alpha, layer, w_in, w_pool, pool_scale, conv_w, w_o,
           ln1_g, ln1_b, ln2_g, ln2_b, w_router, router_bias,
           w_gate, w_up, w_down, ws_gate, ws_up, ws_down):
    batch, seq, d = xp.shape
    bs = xs.shape[0]
    pool_buf, pw = st_pool.shape[1], st_pool.shape[2]
    conv_buf, cw = st_conv.shape[1], st_conv.shape[2]
    n_experts = w_router.shape[1]
    pg = w_pool.shape[1]

    xp2 = xp.reshape(batch * seq, d)
    xs2 = xs.reshape(bs, d)
    w_in16 = w_in.astype(BF16)
    w_pool16 = w_pool.astype(BF16)
    scale2 = pool_scale.reshape(1, pw)

    xp16 = xp2.astype(BF16)
    pool_p, new_pool_p = _pool_prompt(xp16, w_in16, w_pool16, scale2, batch, seq, pool_buf)
    conv_p, new_conv_p = _conv_prompt(xp16, w_in16, conv_w, batch, seq, pw, cw, pg, conv_buf)

    stp_t = jnp.transpose(st_pool, (1, 0, 2))
    stc_t = jnp.transpose(st_conv, (1, 0, 2))
    pool_s, conv_s, u_s, v_s = _mixer_sample(xs2.astype(BF16), w_in16, stp_t, stc_t, conv_w,
                                             w_pool16, scale2)
    new_pool_s = jnp.concatenate([st_pool[:, 1:], u_s[:, None, :]], axis=1)
    new_conv_s = jnp.concatenate([st_conv[:, 1:], v_s[:, None, :]], axis=1)

    x_all = jnp.concatenate([xp2, xs2], axis=0)
    pool_all = jnp.concatenate([pool_p, pool_s], axis=0)
    conv_all = jnp.concatenate([conv_p, conv_s], axis=0)
    h32, h16, hp, scores = _outproj(pool_all, conv_all, w_o.astype(BF16), x_all,
                                    ln1_g.reshape(1, d), ln1_b.reshape(1, d),
                                    w_router.astype(BF16), alpha)

    gates, order, item_e, item_ep, item_start, item_cnt = _route(scores, router_bias, n_experts)
    y_slots = _moe_routed(hp, order, item_e, item_ep, item_start, item_cnt, w_gate, w_up, w_down, layer)
    shared = _shared_ffn(h16, ws_gate.astype(BF16), ws_up.astype(BF16), ws_down.astype(BF16))
    y = _combine_ln(h32, y_slots.reshape(TOP_K, h32.shape[0], d), gates, shared,
                    ln2_g.reshape(1, d), ln2_b.reshape(1, d), alpha)
    yp = y[:batch * seq].reshape(batch, seq, d)
    ys = y[batch * seq:].reshape(bs, xs.shape[1], d)
    return yp, ys, new_pool_p, new_pool_s, new_conv_p, new_conv_s


def kernel(x_prompt, x_sample, state_pool, state_conv, w_in, w_pool, pool_scale, conv_w, w_o,
           ln1_g, ln1_b, ln2_g, ln2_b, w_router, router_bias,
           w_gate, w_up, w_down, ws_gate, ws_up, ws_down):
    depth = w_in.shape[0]
    assert x_sample.shape[1] == 1, "decode rows carry one new token per sequence"
    alpha = (2.0 * depth) ** 0.25
    hp, hs = x_prompt, x_sample
    pool_p, pool_s, conv_p, conv_s = [], [], [], []
    for l in range(depth):
        hp, hs, npp, nps, ncp, ncs = _layer(
            hp, hs, state_pool[l], state_conv[l], alpha, l, w_in[l], w_pool[l], pool_scale[l],
            conv_w[l], w_o[l], ln1_g[l], ln1_b[l], ln2_g[l], ln2_b[l], w_router[l],
            router_bias[l], w_gate, w_up, w_down, ws_gate[l], ws_up[l], ws_down[l])
        pool_p.append(npp)
        pool_s.append(nps)
        conv_p.append(ncp)
        conv_s.append(ncs)
    return (hp, hs, jnp.stack(pool_p, 0), jnp.stack(pool_s, 0), jnp.stack(conv_p, 0), jnp.stack(conv_s, 0))
```

```python
import functools
import math

import jax
import jax.numpy as jnp
from jax import lax
from jax.experimental import pallas as pl
from jax.experimental.pallas import tpu as pltpu

POOL_WINDOWS = (2, 4, 8, 16)
TOP_K = 8
ROUTED_SCALE = 2.5
LN_EPS = 1e-5
PAST_LEN = 16384

LANES = 128
SUBLANES = 8
VMEM_LIMIT = 56 * 1024 * 1024

POOL_HALO = 32
CONV_HALO = 8
ITEM_ROWS = 512
ROW_GRAIN = 64
TOP_K_SHIFT = 3
assert 1 << TOP_K_SHIFT == TOP_K and ITEM_ROWS % ROW_GRAIN == 0

BF16 = jnp.bfloat16
F32 = jnp.float32


def _params(*sem):
    return pltpu.CompilerParams(dimension_semantics=sem, vmem_limit_bytes=VMEM_LIMIT)


def _pick_tile(n, target, mult):
    best = None
    for t in range(mult, min(n, target) + 1, mult):
        if n % t == 0:
            best = t
    assert best is not None, (n, target, mult)
    return best


def _dot(a, b):
    return jnp.dot(a, b, preferred_element_type=F32)


def _silu(x):
    return x * jax.nn.sigmoid(x)


def _pack_halves(x16):
    half = x16.shape[1] // 2
    bits = lax.bitcast_convert_type(x16.astype(F32), jnp.uint32)
    return (bits[:, :half] >> 16) | bits[:, half:]


def _unpack_half(words, high):
    bits = jnp.where(high, words & jnp.uint32(0xFFFF0000), words << 16)
    return lax.bitcast_convert_type(bits, F32).astype(BF16)


def _pool_prompt_kernel(x_ref, w_ref, wp_ref, sc_ref, out_ref, st_ref, u_buf, a_buf, b_buf,
                        *, tm, windows, pool_buf):
    g = pl.program_id(0)
    t = pl.program_id(2)
    nt = pl.num_programs(2)
    halo = POOL_HALO
    rows = halo + tm

    @pl.when(t == 0)
    def _():
        u_buf[0:halo, :] = jnp.zeros((halo, u_buf.shape[1]), F32)

    u = _dot(x_ref[...], w_ref[...])
    u_buf[halo:rows, :] = u
    pos1 = (t * tm + lax.broadcasted_iota(jnp.int32, (tm, 1), 0) + 1).astype(F32)

    for gi, w in enumerate(windows):
        @pl.when(g == gi)
        def _(w=w):
            src, shift, start = u_buf, 1, SUBLANES
            dsts = (a_buf, b_buf)
            lev = 0
            while shift < w:
                dst = dsts[lev % 2]
                n = rows - start
                dst[start:rows, :] = src[start:rows, :] + src[start - shift:start - shift + n, :]
                src, shift, start, lev = dst, shift * 2, start + SUBLANES, lev + 1
            win = src[halo:rows, :]
            inv_cnt = 1.0 / jnp.minimum(pos1, float(w))
            pooled = (win * inv_cnt - u).astype(BF16)
            out_ref[...] = (_dot(pooled, wp_ref[...]) * sc_ref[...]).astype(out_ref.dtype)

    @pl.when(t == nt - 1)
    def _():
        st_ref[...] = u_buf[rows - pool_buf:rows, :]

    u_buf[0:halo, :] = u_buf[tm:rows, :]


def _pool_prompt(x16, w_in16, w_pool16, pool_scale, batch, seq, pool_buf):
    m, d = x16.shape
    ngrp, pg, _ = w_pool16.shape
    pw = ngrp * pg
    assert max(POOL_WINDOWS) <= POOL_HALO and len(POOL_WINDOWS) == ngrp and pg % LANES == 0
    tm = _pick_tile(seq, 1024, 16)
    nt = seq // tm
    kern = functools.partial(_pool_prompt_kernel, tm=tm, windows=POOL_WINDOWS, pool_buf=pool_buf)
    return pl.pallas_call(
        kern,
        grid=(ngrp, batch, nt),
        in_specs=[
            pl.BlockSpec((tm, d), lambda g, b, t: (b * nt + t, 0)),
            pl.BlockSpec((d, pg), lambda g, b, t: (0, g)),
            pl.BlockSpec((None, pg, pg), lambda g, b, t: (g, 0, 0)),
            pl.BlockSpec((1, pg), lambda g, b, t: (0, g)),
        ],
        out_specs=[
            pl.BlockSpec((tm, pg), lambda g, b, t: (b * nt + t, g)),
            pl.BlockSpec((None, pool_buf, pg), lambda g, b, t: (b, 0, g)),
        ],
        out_shape=[
            jax.ShapeDtypeStruct((m, pw), BF16),
            jax.ShapeDtypeStruct((batch, pool_buf, pw), F32),
        ],
        scratch_shapes=[pltpu.VMEM((POOL_HALO + tm, pg), F32)] * 3,
        compiler_params=_params("arbitrary", "arbitrary", "arbitrary"),
        name="pool_prompt",
    )(x16, w_in16, w_pool16, pool_scale)


def _conv_prompt_kernel(x_ref, wb_ref, wc_ref, wh_ref, cw_ref, out_ref, st_ref, v_buf,
                        *, tm, conv_buf):
    t = pl.program_id(2)
    nt = pl.num_programs(2)
    halo = CONV_HALO
    rows = halo + tm

    @pl.when(t == 0)
    def _():
        v_buf[0:halo, :] = jnp.zeros((halo, v_buf.shape[1]), F32)

    x = x_ref[...]
    v = _dot(x, wc_ref[...]) * _dot(x, wh_ref[...])
    v_buf[halo:rows, :] = v
    conv = (cw_ref[0:1, :] * v_buf[halo - 2:rows - 2, :]
            + cw_ref[1:2, :] * v_buf[halo - 1:rows - 1, :]
            + cw_ref[2:3, :] * v)
    out_ref[...] = (_dot(x, wb_ref[...]) * conv).astype(out_ref.dtype)

    @pl.when(t == nt - 1)
    def _():
        st_ref[...] = v_buf[rows - conv_buf:rows, :]

    v_buf[0:halo, :] = v_buf[tm:rows, :]


def _conv_prompt(x16, w_in16, conv_w, batch, seq, pw, cw, ct, conv_buf):
    m, d = x16.shape
    assert conv_w.shape[0] == 3 and conv_buf == 2 and cw % ct == 0 and pw % ct == 0
    nc = cw // ct
    off_b, off_c, off_h = pw // ct, (pw + cw) // ct, (pw + 2 * cw) // ct
    tm = _pick_tile(seq, 1024, 16)
    nt = seq // tm
    kern = functools.partial(_conv_prompt_kernel, tm=tm, conv_buf=conv_buf)
    return pl.pallas_call(
        kern,
        grid=(nc, batch, nt),
        in_specs=[
            pl.BlockSpec((tm, d), lambda c, b, t: (b * nt + t, 0)),
            pl.BlockSpec((d, ct), lambda c, b, t: (0, off_b + c)),
            pl.BlockSpec((d, ct), lambda c, b, t: (0, off_c + c)),
            pl.BlockSpec((d, ct), lambda c, b, t: (0, off_h + c)),
            pl.BlockSpec((3, ct), lambda c, b, t: (0, c)),
        ],
        out_specs=[
            pl.BlockSpec((tm, ct), lambda c, b, t: (b * nt + t, c)),
            pl.BlockSpec((None, conv_buf, ct), lambda c, b, t: (b, 0, c)),
        ],
        out_shape=[
            jax.ShapeDtypeStruct((m, cw), BF16),
            jax.ShapeDtypeStruct((batch, conv_buf, cw), F32),
        ],
        scratch_shapes=[pltpu.VMEM((CONV_HALO + tm, ct), F32)],
        compiler_params=_params("arbitrary", "arbitrary", "arbitrary"),
        name="conv_prompt",
    )(x16, w_in16, w_in16, w_in16, conv_w)


def _mixer_sample_kernel(x_ref, wu_ref, wb_ref, wc_ref, wh_ref, stp_ref, stc_ref, cw_ref, wp_ref,
                         sc_ref, pool_ref, conv_ref, u_ref, v_ref, *, windows, pool_buf):
    c = pl.program_id(0)
    x = x_ref[...]
    u = _dot(x, wu_ref[...])
    u_ref[...] = u
    for gi, w in enumerate(windows):
        @pl.when(c == gi)
        def _(w=w):
            win = u
            for j in range(1, w):
                win = win + stp_ref[pool_buf - j]
            cnt = float(min(PAST_LEN + 1, w))
            pooled = (win / cnt - u).astype(BF16)
            pool_ref[...] = (_dot(pooled, wp_ref[...]) * sc_ref[...]).astype(pool_ref.dtype)

    v = _dot(x, wc_ref[...]) * _dot(x, wh_ref[...])
    v_ref[...] = v
    conv = cw_ref[0:1, :] * stc_ref[0] + cw_ref[1:2, :] * stc_ref[1] + cw_ref[2:3, :] * v
    conv_ref[...] = (_dot(x, wb_ref[...]) * conv).astype(conv_ref.dtype)


def _mixer_sample(x16, w_in16, stp_t, stc_t, conv_w, w_pool16, pool_scale):
    bs, d = x16.shape
    ngrp, pg, _ = w_pool16.shape
    pool_buf, conv_buf = stp_t.shape[0], stc_t.shape[0]
    pw, cw = stp_t.shape[2], stc_t.shape[2]
    assert cw == pw, "decode mixer walks pooling and conv column tiles together"
    kern = functools.partial(_mixer_sample_kernel, windows=POOL_WINDOWS, pool_buf=pool_buf)
    col = lambda off: pl.BlockSpec((d, pg), lambda c: (0, off + c))
    tile = pl.BlockSpec((bs, pg), lambda c: (0, c))
    return pl.pallas_call(
        kern,
        grid=(ngrp,),
        in_specs=[
            pl.BlockSpec((bs, d), lambda c: (0, 0)),
            col(0), col(pw // pg), col((pw + cw) // pg), col((pw + 2 * cw) // pg),
            pl.BlockSpec((pool_buf, bs, pg), lambda c: (0, 0, c)),
            pl.BlockSpec((conv_buf, bs, pg), lambda c: (0, 0, c)),
            pl.BlockSpec((3, pg), lambda c: (0, c)),
            pl.BlockSpec((None, pg, pg), lambda c: (c, 0, 0)),
            pl.BlockSpec((1, pg), lambda c: (0, c)),
        ],
        out_specs=[tile, tile, tile, tile],
        out_shape=[
            jax.ShapeDtypeStruct((bs, pw), BF16),
            jax.ShapeDtypeStruct((bs, cw), BF16),
            jax.ShapeDtypeStruct((bs, pw), F32),
            jax.ShapeDtypeStruct((bs, cw), F32),
        ],
        compiler_params=_params("arbitrary"),
        name="mixer_sample",
    )(x16, w_in16, w_in16, w_in16, w_in16, stp_t, stc_t, conv_w, w_pool16, pool_scale)


def _layer_norm(r, g, b):
    mu = jnp.mean(r, axis=-1, keepdims=True)
    d = r - mu
    var = jnp.mean(d * d, axis=-1, keepdims=True)
    return d * lax.rsqrt(var + LN_EPS) * g + b


def _outproj_kernel(pool_ref, conv_ref, wo_ref, x_ref, g_ref, b_ref, wr_ref,
                    h32_ref, h16_ref, hp_ref, sc_ref, acc, *, alpha, nk_pool):
    k = pl.program_id(1)
    nk = pl.num_programs(1)

    @pl.when(k == 0)
    def _():
        acc[...] = alpha * x_ref[...]

    @pl.when(k < nk_pool)
    def _():
        acc[...] += _dot(pool_ref[...], wo_ref[...])

    @pl.when(k >= nk_pool)
    def _():
        acc[...] += _dot(conv_ref[...], wo_ref[...])

    @pl.when(k == nk - 1)
    def _():
        h = _layer_norm(acc[...], g_ref[...], b_ref[...])
        h32_ref[...] = h
        h16 = h.astype(BF16)
        h16_ref[...] = h16
        hp_ref[...] = _pack_halves(h16)
        sc_ref[...] = jax.nn.sigmoid(_dot(h16, wr_ref[...]))


def _outproj(pool_all, conv_all, w_o16, x_all, ln_g, ln_b, w_r16, alpha):
    m, pw = pool_all.shape
    cw = conv_all.shape[1]
    d = w_o16.shape[1]
    e = w_r16.shape[1]
    tm = _pick_tile(m, 320, 16)
    tk = _pick_tile(math.gcd(pw, cw), 512, LANES)
    nkp, nkc = pw // tk, cw // tk
    kern = functools.partial(_outproj_kernel, alpha=alpha, nk_pool=nkp)
    row = lambda width: pl.BlockSpec((tm, width), lambda i, k: (i, 0))
    vec = pl.BlockSpec((1, d), lambda i, k: (0, 0))
    return pl.pallas_call(
        kern,
        grid=(m // tm, nkp + nkc),
        in_specs=[
            pl.BlockSpec((tm, tk), lambda i, k: (i, jnp.minimum(k, nkp - 1))),
            pl.BlockSpec((tm, tk), lambda i, k: (i, jnp.maximum(k - nkp, 0))),
            pl.BlockSpec((tk, d), lambda i, k: (k, 0)),
            row(d), vec, vec,
            pl.BlockSpec((d, e), lambda i, k: (0, 0)),
        ],
        out_specs=[row(d), row(d), row(d // 2), row(e)],
        out_shape=[
            jax.ShapeDtypeStruct((m, d), F32),
            jax.ShapeDtypeStruct((m, d), BF16),
            jax.ShapeDtypeStruct((m, d // 2), jnp.uint32),
            jax.ShapeDtypeStruct((m, e), F32),
        ],
        scratch_shapes=[pltpu.VMEM((tm, d), F32)],
        compiler_params=_params("arbitrary", "arbitrary"),
        name="outproj_ln_router",
    )(pool_all, conv_all, w_o16, x_all, ln_g, ln_b, w_r16)


def _slot_row(a, m_tok):
    return (a & (TOP_K - 1)) * m_tok + _token_of(a)


def _token_of(a):
    return lax.shift_right_logical(a, TOP_K_SHIFT)


def _moe_kernel(order_ref, e_ref, ep_ref, st_ref, cnt_ref,
                hp_hbm, wg_ref, wu_ref, wd_ref,
                y_hbm,
                xbuf, gu_acc, hb16, ybuf, gsem, ssem,
                *, ka, nb, tk, tn, de, m_tok):
    del e_ref, ep_ref
    i = pl.program_id(0)
    s = pl.program_id(1)
    ni = pl.num_programs(0)
    cnt = cnt_ref[i]
    slot = i % 2
    variant = (cnt + ROW_GRAIN - 1) // ROW_GRAIN

    def gather_copy(item_slot, j, tok):
        return pltpu.make_async_copy(hp_hbm.at[pl.ds(tok, 1), :], xbuf.at[item_slot, pl.ds(j, 1), :],
                                     gsem.at[item_slot])

    def scatter_copy(j, row):
        return pltpu.make_async_copy(ybuf.at[pl.ds(j, 1), :], y_hbm.at[pl.ds(row, 1), :], ssem.at[0])

    def share(count, part, parts):
        chunk = (count + parts - 1) // parts
        lo = jnp.minimum(part * chunk, count)
        return lo, jnp.minimum(lo + chunk, count)

    def start_gather(item, item_slot, lo, hi):
        base = st_ref[item]

        def body(j, carry):
            gather_copy(item_slot, j, _token_of(order_ref[base + j])).start()
            return carry
        lax.fori_loop(lo, hi, body, 0)

    def start_scatter(item, lo, hi):
        base = st_ref[item]

        def body(j, carry):
            scatter_copy(j, _slot_row(order_ref[base + j], m_tok)).start()
            return carry
        lax.fori_loop(lo, hi, body, 0)

    def wait_rows(count, block_copy):
        p = ITEM_ROWS
        while p >= 1:
            pl.when((count & p) != 0)(functools.partial(lambda q: block_copy(q).wait(), p))
            p //= 2

    def wait_gather(item, item_slot):
        wait_rows(cnt_ref[item], lambda p: pltpu.make_async_copy(
            hp_hbm.at[pl.ds(0, p), :], xbuf.at[item_slot, pl.ds(0, p), :], gsem.at[item_slot]))

    def wait_scatter(item):
        wait_rows(cnt_ref[item], lambda p: pltpu.make_async_copy(
            ybuf.at[pl.ds(0, p), :], y_hbm.at[pl.ds(0, p), :], ssem.at[0]))

    def for_variant(fn):
        for v in range(1, ITEM_ROWS // ROW_GRAIN + 1):
            pl.when(variant == v)(functools.partial(fn, v * ROW_GRAIN))

    @pl.when(s == 0)
    def _():
        @pl.when(i == 0)
        def _():
            xbuf[...] = jnp.zeros(xbuf.shape, xbuf.dtype)
            start_gather(0, 0, 0, cnt)

        wait_gather(i, slot)

    @pl.when(i + 1 < ni)
    def _():
        lo, hi = share(cnt_ref[i + 1], s, ka + nb)
        start_gather(i + 1, 1 - slot, lo, hi)

    @pl.when(jnp.logical_and(s < ka, i > 0))
    def _():
        prev = jnp.maximum(i - 1, 0)
        lo, hi = share(cnt_ref[prev], s, ka)
        start_scatter(prev, lo, hi)

    @pl.when(s < ka)
    def _():
        chunks_per_half = ka // 2
        high = s >= chunks_per_half
        coff = pl.multiple_of((s % chunks_per_half) * tk, tk)

        def gate_up(rows):
            x16 = _unpack_half(xbuf[slot, 0:rows, pl.ds(coff, tk)], high)
            g = _dot(x16, wg_ref[...].astype(BF16))
            u = _dot(x16, wu_ref[...].astype(BF16))

            @pl.when(s == 0)
            def _():
                gu_acc[0:rows, 0:de] = g
                gu_acc[0:rows, de:2 * de] = u

            @pl.when(s > 0)
            def _():
                gu_acc[0:rows, 0:de] += g
                gu_acc[0:rows, de:2 * de] += u
        for_variant(gate_up)

    @pl.when(s >= ka)
    def _():
        n = s - ka
        col0 = pl.multiple_of(n * tn, tn)

        @pl.when(n == 0)
        def _():
            @pl.when(i > 0)
            def _():
                wait_scatter(jnp.maximum(i - 1, 0))

            def act(rows):
                hb16[0:rows, :] = (_silu(gu_acc[0:rows, 0:de]) * gu_acc[0:rows, de:2 * de]).astype(BF16)
            for_variant(act)

        def down(rows):
            ybuf[0:rows, pl.ds(col0, tn)] = _dot(hb16[0:rows, :], wd_ref[...].astype(BF16))
        for_variant(down)

        @pl.when(jnp.logical_and(n == nb - 1, i == ni - 1))
        def _():
            start_scatter(i, 0, cnt)
            wait_scatter(i)


def _moe_routed(hp, order, n_items, item_e, item_eprev, item_start, item_cnt, w_gate, w_up, w_down, layer):
    m_tok, dh = hp.shape
    d = 2 * dh
    de = w_gate.shape[-1]
    tk = _pick_tile(dh, min(1024, d // 4), LANES)
    tn = tk
    ka, nb = d // tk, d // tn
    assert de % LANES == 0 and nb >= 2 and ka % 2 == 0

    def wgu_map(i, s, order, e, ep, st, cnt):
        return (layer, e[i], jnp.where(cnt[i] > 0, jnp.minimum(s, ka - 1), ka - 1), 0)

    def wd_map(i, s, order, e, ep, st, cnt):
        live = cnt[i] > 0
        first = s < ka
        ex = jnp.where(first, ep[i], e[i])
        blk = jnp.where(jnp.logical_and(live, jnp.logical_not(first)), s - ka, nb - 1)
        return (layer, ex, 0, blk)

    kern = functools.partial(_moe_kernel, ka=ka, nb=nb, tk=tk, tn=tn, de=de, m_tok=m_tok)
    grid_spec = pltpu.PrefetchScalarGridSpec(
        num_scalar_prefetch=5,
        grid=(n_items[0], ka + nb),
        in_specs=[
            pl.BlockSpec(memory_space=pl.ANY),
            pl.BlockSpec((None, None, tk, de), wgu_map),
            pl.BlockSpec((None, None, tk, de), wgu_map),
            pl.BlockSpec((None, None, de, tn), wd_map),
        ],
        out_specs=pl.BlockSpec(memory_space=pl.ANY),
        scratch_shapes=[
            pltpu.VMEM((2, ITEM_ROWS, dh), jnp.uint32),
            pltpu.VMEM((ITEM_ROWS, 2 * de), F32),
            pltpu.VMEM((ITEM_ROWS, de), BF16),
            pltpu.VMEM((ITEM_ROWS, d), F32),
            pltpu.SemaphoreType.DMA((2,)),
            pltpu.SemaphoreType.DMA((1,)),
        ],
    )
    return pl.pallas_call(
        kern,
        grid_spec=grid_spec,
        out_shape=jax.ShapeDtypeStruct((TOP_K * m_tok, d), F32),
        compiler_params=_params("arbitrary", "arbitrary"),
        name="moe_routed",
    )(order, item_e, item_eprev, item_start, item_cnt, hp, w_gate, w_up, w_down)


def _route(scores, bias, n_experts):
    m = scores.shape[0]
    a = m * TOP_K
    _, idx = lax.top_k(scores + bias[None, :].astype(F32), TOP_K)
    sel = jnp.take_along_axis(scores, idx, axis=1)
    gates = sel / jnp.sum(sel, axis=-1, keepdims=True) * ROUTED_SCALE

    abits = max(1, (a - 1).bit_length())
    assert (n_experts << abits) < 2 ** 31
    e_flat = idx.reshape(-1).astype(jnp.int32)
    keys = lax.sort((e_flat << abits) | jnp.arange(a, dtype=jnp.int32))
    order = keys & ((1 << abits) - 1)
    experts = jnp.arange(n_experts, dtype=jnp.int32)
    counts = jnp.sum((idx.astype(jnp.int32)[:, :, None] == experts[None, None, :]).astype(jnp.int32), axis=(0, 1))
    starts = jnp.cumsum(counts) - counts

    ni = n_experts + a // ITEM_ROWS
    n_items = (counts + ITEM_ROWS - 1) // ITEM_ROWS
    item_end = jnp.cumsum(n_items)
    total = item_end[-1]
    slots = jnp.arange(ni, dtype=jnp.int32)
    live = slots < total
    sl = jnp.minimum(slots, total - 1)
    ex = jnp.sum((item_end[None, :] <= sl[:, None]).astype(jnp.int32), axis=1)
    jj = sl - (item_end[ex] - n_items[ex])
    item_start = starts[ex] + jj * ITEM_ROWS
    item_cnt = jnp.where(live, jnp.minimum(ITEM_ROWS, counts[ex] - jj * ITEM_ROWS), 0)
    ex_prev = jnp.concatenate([ex[:1], ex[:-1]])
    return (gates, order.astype(jnp.int32), total.reshape(1).astype(jnp.int32), ex.astype(jnp.int32),
            ex_prev.astype(jnp.int32), item_start.astype(jnp.int32), item_cnt.astype(jnp.int32))


def _shared_kernel(h_ref, wgu_ref, wd_ref, out_ref, *, ch):
    c = pl.program_id(1)
    gu = _dot(h_ref[...], wgu_ref[...])
    hb = (_silu(gu[:, 0:ch]) * gu[:, ch:2 * ch]).astype(BF16)
    part = _dot(hb, wd_ref[...])

    @pl.when(c == 0)
    def _():
        out_ref[...] = part

    @pl.when(c > 0)
    def _():
        out_ref[...] += part


def _shared_ffn(h16, ws_gate16, ws_up16, ws_down16):
    m, d = h16.shape
    de = ws_gate16.shape[1]
    ch = _pick_tile(de, 384, LANES)
    nc = de // ch
    tm = _pick_tile(m, 320, 16)
    wgu = jnp.concatenate(
        [jnp.concatenate([ws_gate16[:, c * ch:(c + 1) * ch], ws_up16[:, c * ch:(c + 1) * ch]], axis=1)
         for c in range(nc)], axis=1)
    return pl.pallas_call(
        functools.partial(_shared_kernel, ch=ch),
        grid=(m // tm, nc),
        in_specs=[
            pl.BlockSpec((tm, d), lambda i, c: (i, 0)),
            pl.BlockSpec((d, 2 * ch), lambda i, c: (0, c)),
            pl.BlockSpec((ch, d), lambda i, c: (c, 0)),
        ],
        out_specs=pl.BlockSpec((tm, d), lambda i, c: (i, 0)),
        out_shape=jax.ShapeDtypeStruct((m, d), F32),
        compiler_params=_params("arbitrary", "arbitrary"),
        name="shared_ffn",
    )(h16, wgu, ws_down16)


def _combine_ln_kernel(h_ref, y_ref, gt_ref, s_ref, g_ref, b_ref, outp_ref, outs_ref, *, alpha, n_prompt):
    i = pl.program_id(0)
    routed = gt_ref[:, 0:1] * y_ref[0]
    for k in range(1, TOP_K):
        routed = routed + gt_ref[:, k:k + 1] * y_ref[k]
    y = _layer_norm(alpha * h_ref[...] + (routed + s_ref[...]), g_ref[...], b_ref[...])

    @pl.when(i < n_prompt)
    def _():
        outp_ref[...] = y

    @pl.when(i >= n_prompt)
    def _():
        outs_ref[...] = y


def _combine_ln(h32, y_slots, gates, shared, ln_g, ln_b, alpha, m_prompt):
    m, d = h32.shape
    m_sample = m - m_prompt
    tm = _pick_tile(math.gcd(m_prompt, m_sample), 64, SUBLANES)
    n_prompt = m_prompt // tm
    row = pl.BlockSpec((tm, d), lambda i: (i, 0))
    vec = pl.BlockSpec((1, d), lambda i: (0, 0))
    return pl.pallas_call(
        functools.partial(_combine_ln_kernel, alpha=alpha, n_prompt=n_prompt),
        grid=(m // tm,),
        in_specs=[row, pl.BlockSpec((TOP_K, tm, d), lambda i: (0, i, 0)),
                  pl.BlockSpec((tm, TOP_K), lambda i: (i, 0)), row, vec, vec],
        out_specs=[pl.BlockSpec((tm, d), lambda i: (jnp.minimum(i, n_prompt - 1), 0)),
                   pl.BlockSpec((tm, d), lambda i: (jnp.maximum(i - n_prompt, 0), 0))],
        out_shape=[jax.ShapeDtypeStruct((m_prompt, d), F32), jax.ShapeDtypeStruct((m_sample, d), F32)],
        compiler_params=_params("arbitrary"),
        name="combine_ln",
    )(h32, y_slots, gates, shared, ln_g, ln_b)


def _layer(xp, xs, st_pool, st_conv, alpha, layer, w_in, w_pool, pool_scale, conv_w, w_o,
           ln1_g, ln1_b, ln2_g, ln2_b, w_router, router_bias,
           w_gate, w_up, w_down, ws_gate, ws_up, ws_down):
    batch, seq, d = xp.shape
    bs = xs.shape[0]
    pool_buf, pw = st_pool.shape[1], st_pool.shape[2]
    conv_buf, cw = st_conv.shape[1], st_conv.shape[2]
    n_experts = w_router.shape[1]
    pg = w_pool.shape[1]

    xp2 = xp.reshape(batch * seq, d)
    xs2 = xs.reshape(bs, d)
    w_in16 = w_in.astype(BF16)
    w_pool16 = w_pool.astype(BF16)
    scale2 = pool_scale.reshape(1, pw)

    xp16 = xp2.astype(BF16)
    pool_p, new_pool_p = _pool_prompt(xp16, w_in16, w_pool16, scale2, batch, seq, pool_buf)
    conv_p, new_conv_p = _conv_prompt(xp16, w_in16, conv_w, batch, seq, pw, cw, pg, conv_buf)

    stp_t = jnp.transpose(st_pool, (1, 0, 2))
    stc_t = jnp.transpose(st_conv, (1, 0, 2))
    pool_s, conv_s, u_s, v_s = _mixer_sample(xs2.astype(BF16), w_in16, stp_t, stc_t, conv_w,
                                             w_pool16, scale2)
    new_pool_s = jnp.concatenate([st_pool[:, 1:], u_s[:, None, :]], axis=1)
    new_conv_s = jnp.concatenate([st_conv[:, 1:], v_s[:, None, :]], axis=1)

    x_all = jnp.concatenate([xp2, xs2], axis=0)
    pool_all = jnp.concatenate([pool_p, pool_s], axis=0)
    conv_all = jnp.concatenate([conv_p, conv_s], axis=0)
    h32, h16, hp, scores = _outproj(pool_all, conv_all, w_o.astype(BF16), x_all,
                                    ln1_g.reshape(1, d), ln1_b.reshape(1, d),
                                    w_router.astype(BF16), alpha)

    gates, order, n_items, item_e, item_ep, item_start, item_cnt = _route(scores, router_bias, n_experts)
    y_slots = _moe_routed(hp, order, n_items, item_e, item_ep, item_start, item_cnt,
                          w_gate, w_up, w_down, layer)
    shared = _shared_ffn(h16, ws_gate.astype(BF16), ws_up.astype(BF16), ws_down.astype(BF16))
    yp, ys = _combine_ln(h32, y_slots.reshape(TOP_K, h32.shape[0], d), gates, shared,
                         ln2_g.reshape(1, d), ln2_b.reshape(1, d), alpha, batch * seq)
    return (yp.reshape(batch, seq, d), ys.reshape(bs, xs.shape[1], d),
            new_pool_p, new_pool_s, new_conv_p, new_conv_s)


def kernel(x_prompt, x_sample, state_pool, state_conv, w_in, w_pool, pool_scale, conv_w, w_o,
           ln1_g, ln1_b, ln2_g, ln2_b, w_router, router_bias,
           w_gate, w_up, w_down, ws_gate, ws_up, ws_down):
    depth = w_in.shape[0]
    assert x_sample.shape[1] == 1, "decode rows carry one new token per sequence"
    alpha = (2.0 * depth) ** 0.25
    hp, hs = x_prompt, x_sample
    pool_p, pool_s, conv_p, conv_s = [], [], [], []
    for l in range(depth):
        hp, hs, npp, nps, ncp, ncs = _layer(
            hp, hs, state_pool[l], state_conv[l], alpha, l, w_in[l], w_pool[l], pool_scale[l],
            conv_w[l], w_o[l], ln1_g[l], ln1_b[l], ln2_g[l], ln2_b[l], w_router[l],
            router_bias[l], w_gate, w_up, w_down, ws_gate[l], ws_up[l], ws_down[l])
        pool_p.append(npp)
        pool_s.append(nps)
        conv_p.append(ncp)
        conv_s.append(ncs)
    return (hp, hs, jnp.stack(pool_p, 0), jnp.stack(pool_s, 0), jnp.stack(conv_p, 0), jnp.stack(conv_s, 0))
```

```python
import functools
import math

import jax
import jax.numpy as jnp
from jax import lax
from jax.experimental import pallas as pl
from jax.experimental.pallas import tpu as pltpu

POOL_WINDOWS = (2, 4, 8, 16)
TOP_K = 8
ROUTED_SCALE = 2.5
LN_EPS = 1e-5
PAST_LEN = 16384

LANES = 128
SUBLANES = 8
VMEM_LIMIT = 56 * 1024 * 1024

POOL_HALO = 32
CONV_HALO = 8
ITEM_ROWS = 512
ROW_GRAIN = 64
TOP_K_SHIFT = 3
assert 1 << TOP_K_SHIFT == TOP_K and ITEM_ROWS % ROW_GRAIN == 0

BF16 = jnp.bfloat16
F32 = jnp.float32


def _params(*sem):
    return pltpu.CompilerParams(dimension_semantics=sem, vmem_limit_bytes=VMEM_LIMIT)


def _pick_tile(n, target, mult):
    best = None
    for t in range(mult, min(n, target) + 1, mult):
        if n % t == 0:
            best = t
    assert best is not None, (n, target, mult)
    return best


def _dot(a, b):
    return jnp.dot(a, b, preferred_element_type=F32)


def _silu(x):
    return x * jax.nn.sigmoid(x)


def _pack_halves(x16):
    half = x16.shape[1] // 2
    bits = lax.bitcast_convert_type(x16.astype(F32), jnp.uint32)
    return (bits[:, :half] >> 16) | bits[:, half:]


def _unpack_half(words, high):
    bits = jnp.where(high, words & jnp.uint32(0xFFFF0000), words << 16)
    return lax.bitcast_convert_type(bits, F32).astype(BF16)


def _pool_prompt_kernel(x_ref, w_ref, wp_ref, sc_ref, out_ref, st_ref, u_buf, a_buf, b_buf,
                        *, tm, windows, pool_buf):
    g = pl.program_id(0)
    t = pl.program_id(2)
    nt = pl.num_programs(2)
    halo = POOL_HALO
    rows = halo + tm

    @pl.when(t == 0)
    def _():
        u_buf[0:halo, :] = jnp.zeros((halo, u_buf.shape[1]), F32)

    u = _dot(x_ref[...], w_ref[...])
    u_buf[halo:rows, :] = u
    pos1 = (t * tm + lax.broadcasted_iota(jnp.int32, (tm, 1), 0) + 1).astype(F32)

    for gi, w in enumerate(windows):
        @pl.when(g == gi)
        def _(w=w):
            src, shift, start = u_buf, 1, SUBLANES
            dsts = (a_buf, b_buf)
            lev = 0
            while shift < w:
                dst = dsts[lev % 2]
                n = rows - start
                dst[start:rows, :] = src[start:rows, :] + src[start - shift:start - shift + n, :]
                src, shift, start, lev = dst, shift * 2, start + SUBLANES, lev + 1
            win = src[halo:rows, :]
            inv_cnt = 1.0 / jnp.minimum(pos1, float(w))
            pooled = (win * inv_cnt - u).astype(BF16)
            out_ref[...] = (_dot(pooled, wp_ref[...]) * sc_ref[...]).astype(out_ref.dtype)

    @pl.when(t == nt - 1)
    def _():
        st_ref[...] = u_buf[rows - pool_buf:rows, :]

    u_buf[0:halo, :] = u_buf[tm:rows, :]


def _pool_prompt(x16, w_in16, w_pool16, pool_scale, batch, seq, pool_buf):
    m, d = x16.shape
    ngrp, pg, _ = w_pool16.shape
    pw = ngrp * pg
    assert max(POOL_WINDOWS) <= POOL_HALO and len(POOL_WINDOWS) == ngrp and pg % LANES == 0
    tm = _pick_tile(seq, 1024, 16)
    nt = seq // tm
    kern = functools.partial(_pool_prompt_kernel, tm=tm, windows=POOL_WINDOWS, pool_buf=pool_buf)
    return pl.pallas_call(
        kern,
        grid=(ngrp, batch, nt),
        in_specs=[
            pl.BlockSpec((tm, d), lambda g, b, t: (b * nt + t, 0)),
            pl.BlockSpec((d, pg), lambda g, b, t: (0, g)),
            pl.BlockSpec((None, pg, pg), lambda g, b, t: (g, 0, 0)),
            pl.BlockSpec((1, pg), lambda g, b, t: (0, g)),
        ],
        out_specs=[
            pl.BlockSpec((tm, pg), lambda g, b, t: (b * nt + t, g)),
            pl.BlockSpec((None, pool_buf, pg), lambda g, b, t: (b, 0, g)),
        ],
        out_shape=[
            jax.ShapeDtypeStruct((m, pw), BF16),
            jax.ShapeDtypeStruct((batch, pool_buf, pw), F32),
        ],
        scratch_shapes=[pltpu.VMEM((POOL_HALO + tm, pg), F32)] * 3,
        compiler_params=_params("arbitrary", "arbitrary", "arbitrary"),
        name="pool_prompt",
    )(x16, w_in16, w_pool16, pool_scale)


def _conv_prompt_kernel(x_ref, wb_ref, wc_ref, wh_ref, cw_ref, out_ref, st_ref, v_buf,
                        *, tm, conv_buf):
    t = pl.program_id(2)
    nt = pl.num_programs(2)
    halo = CONV_HALO
    rows = halo + tm

    @pl.when(t == 0)
    def _():
        v_buf[0:halo, :] = jnp.zeros((halo, v_buf.shape[1]), F32)

    x = x_ref[...]
    v = _dot(x, wc_ref[...]) * _dot(x, wh_ref[...])
    v_buf[halo:rows, :] = v
    conv = (cw_ref[0:1, :] * v_buf[halo - 2:rows - 2, :]
            + cw_ref[1:2, :] * v_buf[halo - 1:rows - 1, :]
            + cw_ref[2:3, :] * v)
    out_ref[...] = (_dot(x, wb_ref[...]) * conv).astype(out_ref.dtype)

    @pl.when(t == nt - 1)
    def _():
        st_ref[...] = v_buf[rows - conv_buf:rows, :]

    v_buf[0:halo, :] = v_buf[tm:rows, :]


def _conv_prompt(x16, w_in16, conv_w, batch, seq, pw, cw, ct, conv_buf):
    m, d = x16.shape
    assert conv_w.shape[0] == 3 and conv_buf == 2 and cw % ct == 0 and pw % ct == 0
    nc = cw // ct
    off_b, off_c, off_h = pw // ct, (pw + cw) // ct, (pw + 2 * cw) // ct
    tm = _pick_tile(seq, 1024, 16)
    nt = seq // tm
    kern = functools.partial(_conv_prompt_kernel, tm=tm, conv_buf=conv_buf)
    return pl.pallas_call(
        kern,
        grid=(nc, batch, nt),
        in_specs=[
            pl.BlockSpec((tm, d), lambda c, b, t: (b * nt + t, 0)),
            pl.BlockSpec((d, ct), lambda c, b, t: (0, off_b + c)),
            pl.BlockSpec((d, ct), lambda c, b, t: (0, off_c + c)),
            pl.BlockSpec((d, ct), lambda c, b, t: (0, off_h + c)),
            pl.BlockSpec((3, ct), lambda c, b, t: (0, c)),
        ],
        out_specs=[
            pl.BlockSpec((tm, ct), lambda c, b, t: (b * nt + t, c)),
            pl.BlockSpec((None, conv_buf, ct), lambda c, b, t: (b, 0, c)),
        ],
        out_shape=[
            jax.ShapeDtypeStruct((m, cw), BF16),
            jax.ShapeDtypeStruct((batch, conv_buf, cw), F32),
        ],
        scratch_shapes=[pltpu.VMEM((CONV_HALO + tm, ct), F32)],
        compiler_params=_params("arbitrary", "arbitrary", "arbitrary"),
        name="conv_prompt",
    )(x16, w_in16, w_in16, w_in16, conv_w)


def _mixer_sample_kernel(x_ref, wu_ref, wb_ref, wc_ref, wh_ref, stp_ref, stc_ref, cw_ref, wp_ref,
                         sc_ref, pool_ref, conv_ref, u_ref, v_ref, *, windows, pool_buf):
    c = pl.program_id(0)
    x = x_ref[...]
    u = _dot(x, wu_ref[...])
    u_ref[...] = u
    for gi, w in enumerate(windows):
        @pl.when(c == gi)
        def _(w=w):
            win = u
            for j in range(1, w):
                win = win + stp_ref[pool_buf - j]
            cnt = float(min(PAST_LEN + 1, w))
            pooled = (win / cnt - u).astype(BF16)
            pool_ref[...] = (_dot(pooled, wp_ref[...]) * sc_ref[...]).astype(pool_ref.dtype)

    v = _dot(x, wc_ref[...]) * _dot(x, wh_ref[...])
    v_ref[...] = v
    conv = cw_ref[0:1, :] * stc_ref[0] + cw_ref[1:2, :] * stc_ref[1] + cw_ref[2:3, :] * v
    conv_ref[...] = (_dot(x, wb_ref[...]) * conv).astype(conv_ref.dtype)


def _mixer_sample(x16, w_in16, stp_t, stc_t, conv_w, w_pool16, pool_scale):
    bs, d = x16.shape
    ngrp, pg, _ = w_pool16.shape
    pool_buf, conv_buf = stp_t.shape[0], stc_t.shape[0]
    pw, cw = stp_t.shape[2], stc_t.shape[2]
    assert cw == pw, "decode mixer walks pooling and conv column tiles together"
    kern = functools.partial(_mixer_sample_kernel, windows=POOL_WINDOWS, pool_buf=pool_buf)
    col = lambda off: pl.BlockSpec((d, pg), lambda c: (0, off + c))
    tile = pl.BlockSpec((bs, pg), lambda c: (0, c))
    return pl.pallas_call(
        kern,
        grid=(ngrp,),
        in_specs=[
            pl.BlockSpec((bs, d), lambda c: (0, 0)),
            col(0), col(pw // pg), col((pw + cw) // pg), col((pw + 2 * cw) // pg),
            pl.BlockSpec((pool_buf, bs, pg), lambda c: (0, 0, c)),
            pl.BlockSpec((conv_buf, bs, pg), lambda c: (0, 0, c)),
            pl.BlockSpec((3, pg), lambda c: (0, c)),
            pl.BlockSpec((None, pg, pg), lambda c: (c, 0, 0)),
            pl.BlockSpec((1, pg), lambda c: (0, c)),
        ],
        out_specs=[tile, tile, tile, tile],
        out_shape=[
            jax.ShapeDtypeStruct((bs, pw), BF16),
            jax.ShapeDtypeStruct((bs, cw), BF16),
            jax.ShapeDtypeStruct((bs, pw), F32),
            jax.ShapeDtypeStruct((bs, cw), F32),
        ],
        compiler_params=_params("arbitrary"),
        name="mixer_sample",
    )(x16, w_in16, w_in16, w_in16, w_in16, stp_t, stc_t, conv_w, w_pool16, pool_scale)


def _layer_norm(r, g, b):
    mu = jnp.mean(r, axis=-1, keepdims=True)
    d = r - mu
    var = jnp.mean(d * d, axis=-1, keepdims=True)
    return d * lax.rsqrt(var + LN_EPS) * g + b


def _outproj_kernel(pool_ref, conv_ref, wo_ref, x_ref, g_ref, b_ref, wr_ref,
                    h32_ref, h16_ref, hp_ref, sc_ref, acc, *, alpha, nk_pool):
    k = pl.program_id(1)
    nk = pl.num_programs(1)

    @pl.when(k == 0)
    def _():
        acc[...] = alpha * x_ref[...]

    @pl.when(k < nk_pool)
    def _():
        acc[...] += _dot(pool_ref[...], wo_ref[...])

    @pl.when(k >= nk_pool)
    def _():
        acc[...] += _dot(conv_ref[...], wo_ref[...])

    @pl.when(k == nk - 1)
    def _():
        h = _layer_norm(acc[...], g_ref[...], b_ref[...])
        h32_ref[...] = h
        h16 = h.astype(BF16)
        h16_ref[...] = h16
        hp_ref[...] = _pack_halves(h16)
        sc_ref[...] = jax.nn.sigmoid(_dot(h16, wr_ref[...]))


def _outproj(pool_all, conv_all, w_o16, x_all, ln_g, ln_b, w_r16, alpha):
    m, pw = pool_all.shape
    cw = conv_all.shape[1]
    d = w_o16.shape[1]
    e = w_r16.shape[1]
    tm = _pick_tile(m, 320, 16)
    tk = _pick_tile(math.gcd(pw, cw), 512, LANES)
    nkp, nkc = pw // tk, cw // tk
    kern = functools.partial(_outproj_kernel, alpha=alpha, nk_pool=nkp)
    row = lambda width: pl.BlockSpec((tm, width), lambda i, k: (i, 0))
    vec = pl.BlockSpec((1, d), lambda i, k: (0, 0))
    return pl.pallas_call(
        kern,
        grid=(m // tm, nkp + nkc),
        in_specs=[
            pl.BlockSpec((tm, tk), lambda i, k: (i, jnp.minimum(k, nkp - 1))),
            pl.BlockSpec((tm, tk), lambda i, k: (i, jnp.maximum(k - nkp, 0))),
            pl.BlockSpec((tk, d), lambda i, k: (k, 0)),
            row(d), vec, vec,
            pl.BlockSpec((d, e), lambda i, k: (0, 0)),
        ],
        out_specs=[row(d), row(d), row(d // 2), row(e)],
        out_shape=[
            jax.ShapeDtypeStruct((m, d), F32),
            jax.ShapeDtypeStruct((m, d), BF16),
            jax.ShapeDtypeStruct((m, d // 2), jnp.uint32),
            jax.ShapeDtypeStruct((m, e), F32),
        ],
        scratch_shapes=[pltpu.VMEM((tm, d), F32)],
        compiler_params=_params("arbitrary", "arbitrary"),
        name="outproj_ln_router",
    )(pool_all, conv_all, w_o16, x_all, ln_g, ln_b, w_r16)


def _slot_row(a, m_tok):
    return (a & (TOP_K - 1)) * m_tok + _token_of(a)


def _token_of(a):
    return lax.shift_right_logical(a, TOP_K_SHIFT)


def _moe_kernel(order_ref, e_ref, st_ref, cnt_ref,
                hp_hbm, wg_hbm, wu_hbm, wd_hbm,
                y_hbm,
                xbuf, gu_acc, hb16, ybuf, wg_buf, wu_buf, wd_buf, gsem, ssem, wsem,
                *, ka, nb, tk, tn, de, m_tok, layer):
    i = pl.program_id(0)
    s = pl.program_id(1)
    ni = pl.num_programs(0)
    nsteps = ka + nb
    cnt = cnt_ref[i]
    slot = i % 2
    variant = (cnt + ROW_GRAIN - 1) // ROW_GRAIN

    def for_weight_copies(item, step, fn):
        e = e_ref[item]

        @pl.when(step < ka)
        def _():
            ws = step % 2
            rows = pl.ds(pl.multiple_of(step * tk, tk), tk)
            fn(pltpu.make_async_copy(wg_hbm.at[layer, e, rows, :], wg_buf.at[ws], wsem.at[0, ws]), 0)
            fn(pltpu.make_async_copy(wu_hbm.at[layer, e, rows, :], wu_buf.at[ws], wsem.at[1, ws]), 1)

        @pl.when(step >= ka)
        def _():
            blk = step - ka
            ws = blk % 2
            cols = pl.ds(pl.multiple_of(blk * tn, tn), tn)
            half = de // 2
            for part in range(2):
                r = pl.ds(part * half, half)
                fn(pltpu.make_async_copy(wd_hbm.at[layer, e, r, cols], wd_buf.at[ws, r, :],
                                         wsem.at[2 + part, ws]), part)

    @pl.when(jnp.logical_and(i == 0, s == 0))
    def _():
        for_weight_copies(0, 0, lambda c, p: c.start(priority=p))

    last_step = s + 1 == nsteps
    nxt_i = jnp.where(last_step, i + 1, i)

    @pl.when(nxt_i < ni)
    def _():
        for_weight_copies(nxt_i, jnp.where(last_step, 0, s + 1), lambda c, p: c.start(priority=p))

    for_weight_copies(i, s, lambda c, p: c.wait())

    def gather_copy(item_slot, j, tok):
        return pltpu.make_async_copy(hp_hbm.at[pl.ds(tok, 1), :], xbuf.at[item_slot, pl.ds(j, 1), :],
                                     gsem.at[item_slot])

    def scatter_copy(j, row):
        return pltpu.make_async_copy(ybuf.at[pl.ds(j, 1), :], y_hbm.at[pl.ds(row, 1), :], ssem.at[0])

    def share(count, part, parts):
        chunk = (count + parts - 1) // parts
        lo = jnp.minimum(part * chunk, count)
        return lo, jnp.minimum(lo + chunk, count)

    def start_gather(item, item_slot, lo, hi):
        base = st_ref[item]

        def body(j, carry):
            gather_copy(item_slot, j, _token_of(order_ref[base + j])).start()
            return carry
        lax.fori_loop(lo, hi, body, 0)

    def start_scatter(item, lo, hi):
        base = st_ref[item]

        def body(j, carry):
            scatter_copy(j, _slot_row(order_ref[base + j], m_tok)).start()
            return carry
        lax.fori_loop(lo, hi, body, 0)

    def wait_rows(count, block_copy):
        p = ITEM_ROWS
        while p >= 1:
            pl.when((count & p) != 0)(functools.partial(lambda q: block_copy(q).wait(), p))
            p //= 2

    def wait_gather(item, item_slot):
        wait_rows(cnt_ref[item], lambda p: pltpu.make_async_copy(
            hp_hbm.at[pl.ds(0, p), :], xbuf.at[item_slot, pl.ds(0, p), :], gsem.at[item_slot]))

    def wait_scatter(item):
        wait_rows(cnt_ref[item], lambda p: pltpu.make_async_copy(
            ybuf.at[pl.ds(0, p), :], y_hbm.at[pl.ds(0, p), :], ssem.at[0]))

    def for_variant(fn):
        for v in range(1, ITEM_ROWS // ROW_GRAIN + 1):
            pl.when(variant == v)(functools.partial(fn, v * ROW_GRAIN))

    @pl.when(s == 0)
    def _():
        @pl.when(i == 0)
        def _():
            xbuf[...] = jnp.zeros(xbuf.shape, xbuf.dtype)
            start_gather(0, 0, 0, cnt)

        wait_gather(i, slot)

    @pl.when(i + 1 < ni)
    def _():
        lo, hi = share(cnt_ref[i + 1], s, ka + nb)
        start_gather(i + 1, 1 - slot, lo, hi)

    @pl.when(jnp.logical_and(s < ka, i > 0))
    def _():
        prev = jnp.maximum(i - 1, 0)
        lo, hi = share(cnt_ref[prev], s, ka)
        start_scatter(prev, lo, hi)

    @pl.when(s < ka)
    def _():
        chunks_per_half = ka // 2
        high = s >= chunks_per_half
        coff = pl.multiple_of((s % chunks_per_half) * tk, tk)

        def gate_up(rows):
            x16 = _unpack_half(xbuf[slot, 0:rows, pl.ds(coff, tk)], high)
            g = _dot(x16, wg_buf[s % 2].astype(BF16))
            u = _dot(x16, wu_buf[s % 2].astype(BF16))

            @pl.when(s == 0)
            def _():
                gu_acc[0:rows, 0:de] = g
                gu_acc[0:rows, de:2 * de] = u

            @pl.when(s > 0)
            def _():
                gu_acc[0:rows, 0:de] += g
                gu_acc[0:rows, de:2 * de] += u
        for_variant(gate_up)

    @pl.when(s >= ka)
    def _():
        n = s - ka
        col0 = pl.multiple_of(n * tn, tn)

        @pl.when(n == 0)
        def _():
            @pl.when(i > 0)
            def _():
                wait_scatter(jnp.maximum(i - 1, 0))

            def act(rows):
                hb16[0:rows, :] = (_silu(gu_acc[0:rows, 0:de]) * gu_acc[0:rows, de:2 * de]).astype(BF16)
            for_variant(act)

        def down(rows):
            ybuf[0:rows, pl.ds(col0, tn)] = _dot(hb16[0:rows, :], wd_buf[n % 2].astype(BF16))
        for_variant(down)

        @pl.when(jnp.logical_and(n == nb - 1, i == ni - 1))
        def _():
            start_scatter(i, 0, cnt)
            wait_scatter(i)


def _moe_routed(hp, order, n_items, item_e, item_start, item_cnt, w_gate, w_up, w_down, layer):
    m_tok, dh = hp.shape
    d = 2 * dh
    de = w_gate.shape[-1]
    tk = _pick_tile(dh, min(1024, d // 4), LANES)
    tn = tk
    ka, nb = d // tk, d // tn
    assert de % (2 * SUBLANES) == 0 and de % LANES == 0 and nb >= 2 and ka % 2 == 0 and nb % 2 == 0

    kern = functools.partial(_moe_kernel, ka=ka, nb=nb, tk=tk, tn=tn, de=de, m_tok=m_tok, layer=layer)
    hbm = pl.BlockSpec(memory_space=pl.ANY)
    grid_spec = pltpu.PrefetchScalarGridSpec(
        num_scalar_prefetch=4,
        grid=(n_items[0], ka + nb),
        in_specs=[hbm, hbm, hbm, hbm],
        out_specs=hbm,
        scratch_shapes=[
            pltpu.VMEM((2, ITEM_ROWS, dh), jnp.uint32),
            pltpu.VMEM((ITEM_ROWS, 2 * de), F32),
            pltpu.VMEM((ITEM_ROWS, de), BF16),
            pltpu.VMEM((ITEM_ROWS, d), F32),
            pltpu.VMEM((2, tk, de), F32),
            pltpu.VMEM((2, tk, de), F32),
            pltpu.VMEM((2, de, tn), F32),
            pltpu.SemaphoreType.DMA((2,)),
            pltpu.SemaphoreType.DMA((1,)),
            pltpu.SemaphoreType.DMA((4, 2)),
        ],
    )
    return pl.pallas_call(
        kern,
        grid_spec=grid_spec,
        out_shape=jax.ShapeDtypeStruct((TOP_K * m_tok, d), F32),
        compiler_params=_params("arbitrary", "arbitrary"),
        name="moe_routed",
    )(order, item_e, item_start, item_cnt, hp, w_gate, w_up, w_down)


def _route(scores, bias, n_experts):
    m = scores.shape[0]
    a = m * TOP_K
    _, idx = lax.top_k(scores + bias[None, :].astype(F32), TOP_K)
    sel = jnp.take_along_axis(scores, idx, axis=1)
    gates = sel / jnp.sum(sel, axis=-1, keepdims=True) * ROUTED_SCALE

    abits = max(1, (a - 1).bit_length())
    assert (n_experts << abits) < 2 ** 31
    e_flat = idx.reshape(-1).astype(jnp.int32)
    keys = lax.sort((e_flat << abits) | jnp.arange(a, dtype=jnp.int32))
    order = keys & ((1 << abits) - 1)
    experts = jnp.arange(n_experts, dtype=jnp.int32)
    counts = jnp.sum((idx.astype(jnp.int32)[:, :, None] == experts[None, None, :]).astype(jnp.int32), axis=(0, 1))
    starts = jnp.cumsum(counts) - counts

    ni = n_experts + a // ITEM_ROWS
    n_items = (counts + ITEM_ROWS - 1) // ITEM_ROWS
    item_end = jnp.cumsum(n_items)
    total = item_end[-1]
    slots = jnp.arange(ni, dtype=jnp.int32)
    live = slots < total
    sl = jnp.minimum(slots, total - 1)
    ex = jnp.sum((item_end[None, :] <= sl[:, None]).astype(jnp.int32), axis=1)
    jj = sl - (item_end[ex] - n_items[ex])
    item_start = starts[ex] + jj * ITEM_ROWS
    item_cnt = jnp.where(live, jnp.minimum(ITEM_ROWS, counts[ex] - jj * ITEM_ROWS), 0)
    return (gates, order.astype(jnp.int32), total.reshape(1).astype(jnp.int32), ex.astype(jnp.int32),
            item_start.astype(jnp.int32), item_cnt.astype(jnp.int32))


def _shared_kernel(h_ref, wgu_ref, wd_ref, out_ref, *, ch):
    c = pl.program_id(1)
    gu = _dot(h_ref[...], wgu_ref[...])
    hb = (_silu(gu[:, 0:ch]) * gu[:, ch:2 * ch]).astype(BF16)
    part = _dot(hb, wd_ref[...])

    @pl.when(c == 0)
    def _():
        out_ref[...] = part

    @pl.when(c > 0)
    def _():
        out_ref[...] += part


def _shared_ffn(h16, ws_gate16, ws_up16, ws_down16):
    m, d = h16.shape
    de = ws_gate16.shape[1]
    ch = _pick_tile(de, 384, LANES)
    nc = de // ch
    tm = _pick_tile(m, 320, 16)
    wgu = jnp.concatenate(
        [jnp.concatenate([ws_gate16[:, c * ch:(c + 1) * ch], ws_up16[:, c * ch:(c + 1) * ch]], axis=1)
         for c in range(nc)], axis=1)
    return pl.pallas_call(
        functools.partial(_shared_kernel, ch=ch),
        grid=(m // tm, nc),
        in_specs=[
            pl.BlockSpec((tm, d), lambda i, c: (i, 0)),
            pl.BlockSpec((d, 2 * ch), lambda i, c: (0, c)),
            pl.BlockSpec((ch, d), lambda i, c: (c, 0)),
        ],
        out_specs=pl.BlockSpec((tm, d), lambda i, c: (i, 0)),
        out_shape=jax.ShapeDtypeStruct((m, d), F32),
        compiler_params=_params("arbitrary", "arbitrary"),
        name="shared_ffn",
    )(h16, wgu, ws_down16)


def _combine_ln_kernel(h_ref, y_ref, gt_ref, s_ref, g_ref, b_ref, outp_ref, outs_ref, *, alpha, n_prompt):
    i = pl.program_id(0)
    routed = gt_ref[:, 0:1] * y_ref[0]
    for k in range(1, TOP_K):
        routed = routed + gt_ref[:, k:k + 1] * y_ref[k]
    y = _layer_norm(alpha * h_ref[...] + (routed + s_ref[...]), g_ref[...], b_ref[...])

    @pl.when(i < n_prompt)
    def _():
        outp_ref[...] = y

    @pl.when(i >= n_prompt)
    def _():
        outs_ref[...] = y


def _combine_ln(h32, y_slots, gates, shared, ln_g, ln_b, alpha, m_prompt):
    m, d = h32.shape
    m_sample = m - m_prompt
    tm = _pick_tile(math.gcd(m_prompt, m_sample), 64, SUBLANES)
    n_prompt = m_prompt // tm
    row = pl.BlockSpec((tm, d), lambda i: (i, 0))
    vec = pl.BlockSpec((1, d), lambda i: (0, 0))
    return pl.pallas_call(
        functools.partial(_combine_ln_kernel, alpha=alpha, n_prompt=n_prompt),
        grid=(m // tm,),
        in_specs=[row, pl.BlockSpec((TOP_K, tm, d), lambda i: (0, i, 0)),
                  pl.BlockSpec((tm, TOP_K), lambda i: (i, 0)), row, vec, vec],
        out_specs=[pl.BlockSpec((tm, d), lambda i: (jnp.minimum(i, n_prompt - 1), 0)),
                   pl.BlockSpec((tm, d), lambda i: (jnp.maximum(i - n_prompt, 0), 0))],
        out_shape=[jax.ShapeDtypeStruct((m_prompt, d), F32), jax.ShapeDtypeStruct((m_sample, d), F32)],
        compiler_params=_params("arbitrary"),
        name="combine_ln",
    )(h32, y_slots, gates, shared, ln_g, ln_b)


def _layer(xp, xs, st_pool, st_conv, alpha, layer, w_in, w_pool, pool_scale, conv_w, w_o,
           ln1_g, ln1_b, ln2_g, ln2_b, w_router, router_bias,
           w_gate, w_up, w_down, ws_gate, ws_up, ws_down):
    batch, seq, d = xp.shape
    bs = xs.shape[0]
    pool_buf, pw = st_pool.shape[1], st_pool.shape[2]
    conv_buf, cw = st_conv.shape[1], st_conv.shape[2]
    n_experts = w_router.shape[1]
    pg = w_pool.shape[1]

    xp2 = xp.reshape(batch * seq, d)
    xs2 = xs.reshape(bs, d)
    w_in16 = w_in.astype(BF16)
    w_pool16 = w_pool.astype(BF16)
    scale2 = pool_scale.reshape(1, pw)

    xp16 = xp2.astype(BF16)
    pool_p, new_pool_p = _pool_prompt(xp16, w_in16, w_pool16, scale2, batch, seq, pool_buf)
    conv_p, new_conv_p = _conv_prompt(xp16, w_in16, conv_w, batch, seq, pw, cw, pg, conv_buf)

    stp_t = jnp.transpose(st_pool, (1, 0, 2))
    stc_t = jnp.transpose(st_conv, (1, 0, 2))
    pool_s, conv_s, u_s, v_s = _mixer_sample(xs2.astype(BF16), w_in16, stp_t, stc_t, conv_w,
                                             w_pool16, scale2)
    new_pool_s = jnp.concatenate([st_pool[:, 1:], u_s[:, None, :]], axis=1)
    new_conv_s = jnp.concatenate([st_conv[:, 1:], v_s[:, None, :]], axis=1)

    x_all = jnp.concatenate([xp2, xs2], axis=0)
    pool_all = jnp.concatenate([pool_p, pool_s], axis=0)
    conv_all = jnp.concatenate([conv_p, conv_s], axis=0)
    h32, h16, hp, scores = _outproj(pool_all, conv_all, w_o.astype(BF16), x_all,
                                    ln1_g.reshape(1, d), ln1_b.reshape(1, d),
                                    w_router.astype(BF16), alpha)

    gates, order, n_items, item_e, item_start, item_cnt = _route(scores, router_bias, n_experts)
    y_slots = _moe_routed(hp, order, n_items, item_e, item_start, item_cnt,
                          w_gate, w_up, w_down, layer)
    shared = _shared_ffn(h16, ws_gate.astype(BF16), ws_up.astype(BF16), ws_down.astype(BF16))
    yp, ys = _combine_ln(h32, y_slots.reshape(TOP_K, h32.shape[0], d), gates, shared,
                         ln2_g.reshape(1, d), ln2_b.reshape(1, d), alpha, batch * seq)
    return (yp.reshape(batch, seq, d), ys.reshape(bs, xs.shape[1], d),
            new_pool_p, new_pool_s, new_conv_p, new_conv_s)


def kernel(x_prompt, x_sample, state_pool, state_conv, w_in, w_pool, pool_scale, conv_w, w_o,
           ln1_g, ln1_b, ln2_g, ln2_b, w_router, router_bias,
           w_gate, w_up, w_down, ws_gate, ws_up, ws_down):
    depth = w_in.shape[0]
    assert x_sample.shape[1] == 1, "decode rows carry one new token per sequence"
    alpha = (2.0 * depth) ** 0.25
    hp, hs = x_prompt, x_sample
    pool_p, pool_s, conv_p, conv_s = [], [], [], []
    for l in range(depth):
        hp, hs, npp, nps, ncp, ncs = _layer(
            hp, hs, state_pool[l], state_conv[l], alpha, l, w_in[l], w_pool[l], pool_scale[l],
            conv_w[l], w_o[l], ln1_g[l], ln1_b[l], ln2_g[l], ln2_b[l], w_router[l],
            router_bias[l], w_gate, w_up, w_down, ws_gate[l], ws_up[l], ws_down[l])
        pool_p.append(npp)
        pool_s.append(nps)
        conv_p.append(ncp)
        conv_s.append(ncs)
    return (hp, hs, jnp.stack(pool_p, 0), jnp.stack(pool_s, 0), jnp.stack(conv_p, 0), jnp.stack(conv_s, 0))
```

```python
import functools
import math

import jax
import jax.numpy as jnp
from jax import lax
from jax.experimental import pallas as pl
from jax.experimental.pallas import tpu as pltpu

POOL_WINDOWS = (2, 4, 8, 16)
TOP_K = 8
ROUTED_SCALE = 2.5
LN_EPS = 1e-5
PAST_LEN = 16384

LANES = 128
SUBLANES = 8
VMEM_LIMIT = 56 * 1024 * 1024

POOL_HALO = 32
CONV_HALO = 8
ITEM_ROWS = 512
ROW_GRAIN = 64
TOP_K_SHIFT = 3
assert 1 << TOP_K_SHIFT == TOP_K and ITEM_ROWS % ROW_GRAIN == 0

BF16 = jnp.bfloat16
F32 = jnp.float32


def _params(*sem):
    return pltpu.CompilerParams(dimension_semantics=sem, vmem_limit_bytes=VMEM_LIMIT)


def _pick_tile(n, target, mult):
    best = None
    for t in range(mult, min(n, target) + 1, mult):
        if n % t == 0:
            best = t
    assert best is not None, (n, target, mult)
    return best


def _dot(a, b):
    return jnp.dot(a, b, preferred_element_type=F32)


def _silu(x):
    return x * jax.nn.sigmoid(x)


def _pack_halves(x16):
    half = x16.shape[1] // 2
    bits = lax.bitcast_convert_type(x16.astype(F32), jnp.uint32)
    return (bits[:, :half] >> 16) | bits[:, half:]


def _unpack_half(words, high):
    bits = jnp.where(high, words & jnp.uint32(0xFFFF0000), words << 16)
    return lax.bitcast_convert_type(bits, F32).astype(BF16)


def _pool_prompt_kernel(x_ref, w_ref, wp_ref, sc_ref, out_ref, st_ref, u_buf, a_buf, b_buf,
                        *, tm, windows, pool_buf):
    g = pl.program_id(0)
    t = pl.program_id(2)
    nt = pl.num_programs(2)
    halo = POOL_HALO
    rows = halo + tm

    @pl.when(t == 0)
    def _():
        u_buf[0:halo, :] = jnp.zeros((halo, u_buf.shape[1]), F32)

    u = _dot(x_ref[...], w_ref[...])
    u_buf[halo:rows, :] = u
    pos1 = (t * tm + lax.broadcasted_iota(jnp.int32, (tm, 1), 0) + 1).astype(F32)

    for gi, w in enumerate(windows):
        @pl.when(g == gi)
        def _(w=w):
            src, shift, start = u_buf, 1, SUBLANES
            dsts = (a_buf, b_buf)
            lev = 0
            while shift < w:
                dst = dsts[lev % 2]
                n = rows - start
                dst[start:rows, :] = src[start:rows, :] + src[start - shift:start - shift + n, :]
                src, shift, start, lev = dst, shift * 2, start + SUBLANES, lev + 1
            win = src[halo:rows, :]
            inv_cnt = 1.0 / jnp.minimum(pos1, float(w))
            pooled = (win * inv_cnt - u).astype(BF16)
            out_ref[...] = (_dot(pooled, wp_ref[...]) * sc_ref[...]).astype(out_ref.dtype)

    @pl.when(t == nt - 1)
    def _():
        st_ref[...] = u_buf[rows - pool_buf:rows, :]

    u_buf[0:halo, :] = u_buf[tm:rows, :]


def _pool_prompt(x16, w_in16, w_pool16, pool_scale, batch, seq, pool_buf):
    m, d = x16.shape
    ngrp, pg, _ = w_pool16.shape
    pw = ngrp * pg
    assert max(POOL_WINDOWS) <= POOL_HALO and len(POOL_WINDOWS) == ngrp and pg % LANES == 0
    tm = _pick_tile(seq, 1024, 16)
    nt = seq // tm
    kern = functools.partial(_pool_prompt_kernel, tm=tm, windows=POOL_WINDOWS, pool_buf=pool_buf)
    return pl.pallas_call(
        kern,
        grid=(ngrp, batch, nt),
        in_specs=[
            pl.BlockSpec((tm, d), lambda g, b, t: (b * nt + t, 0)),
            pl.BlockSpec((d, pg), lambda g, b, t: (0, g)),
            pl.BlockSpec((None, pg, pg), lambda g, b, t: (g, 0, 0)),
            pl.BlockSpec((1, pg), lambda g, b, t: (0, g)),
        ],
        out_specs=[
            pl.BlockSpec((tm, pg), lambda g, b, t: (b * nt + t, g)),
            pl.BlockSpec((None, pool_buf, pg), lambda g, b, t: (b, 0, g)),
        ],
        out_shape=[
            jax.ShapeDtypeStruct((m, pw), BF16),
            jax.ShapeDtypeStruct((batch, pool_buf, pw), F32),
        ],
        scratch_shapes=[pltpu.VMEM((POOL_HALO + tm, pg), F32)] * 3,
        compiler_params=_params("arbitrary", "arbitrary", "arbitrary"),
        name="pool_prompt",
    )(x16, w_in16, w_pool16, pool_scale)


def _conv_prompt_kernel(x_ref, wb_ref, wc_ref, wh_ref, cw_ref, out_ref, st_ref, v_buf,
                        *, tm, conv_buf):
    t = pl.program_id(2)
    nt = pl.num_programs(2)
    halo = CONV_HALO
    rows = halo + tm

    @pl.when(t == 0)
    def _():
        v_buf[0:halo, :] = jnp.zeros((halo, v_buf.shape[1]), F32)

    x = x_ref[...]
    v = _dot(x, wc_ref[...]) * _dot(x, wh_ref[...])
    v_buf[halo:rows, :] = v
    conv = (cw_ref[0:1, :] * v_buf[halo - 2:rows - 2, :]
            + cw_ref[1:2, :] * v_buf[halo - 1:rows - 1, :]
            + cw_ref[2:3, :] * v)
    out_ref[...] = (_dot(x, wb_ref[...]) * conv).astype(out_ref.dtype)

    @pl.when(t == nt - 1)
    def _():
        st_ref[...] = v_buf[rows - conv_buf:rows, :]

    v_buf[0:halo, :] = v_buf[tm:rows, :]


def _conv_prompt(x16, w_in16, conv_w, batch, seq, pw, cw, ct, conv_buf):
    m, d = x16.shape
    assert conv_w.shape[0] == 3 and conv_buf == 2 and cw % ct == 0 and pw % ct == 0
    nc = cw // ct
    off_b, off_c, off_h = pw // ct, (pw + cw) // ct, (pw + 2 * cw) // ct
    tm = _pick_tile(seq, 1024, 16)
    nt = seq // tm
    kern = functools.partial(_conv_prompt_kernel, tm=tm, conv_buf=conv_buf)
    return pl.pallas_call(
        kern,
        grid=(nc, batch, nt),
        in_specs=[
            pl.BlockSpec((tm, d), lambda c, b, t: (b * nt + t, 0)),
            pl.BlockSpec((d, ct), lambda c, b, t: (0, off_b + c)),
            pl.BlockSpec((d, ct), lambda c, b, t: (0, off_c + c)),
            pl.BlockSpec((d, ct), lambda c, b, t: (0, off_h + c)),
            pl.BlockSpec((3, ct), lambda c, b, t: (0, c)),
        ],
        out_specs=[
            pl.BlockSpec((tm, ct), lambda c, b, t: (b * nt + t, c)),
            pl.BlockSpec((None, conv_buf, ct), lambda c, b, t: (b, 0, c)),
        ],
        out_shape=[
            jax.ShapeDtypeStruct((m, cw), BF16),
            jax.ShapeDtypeStruct((batch, conv_buf, cw), F32),
        ],
        scratch_shapes=[pltpu.VMEM((CONV_HALO + tm, ct), F32)],
        compiler_params=_params("arbitrary", "arbitrary", "arbitrary"),
        name="conv_prompt",
    )(x16, w_in16, w_in16, w_in16, conv_w)


def _mixer_sample_kernel(x_ref, wu_ref, wb_ref, wc_ref, wh_ref, stp_ref, stc_ref, cw_ref, wp_ref,
                         sc_ref, pool_ref, conv_ref, u_ref, v_ref, *, windows, pool_buf):
    c = pl.program_id(0)
    x = x_ref[...]
    u = _dot(x, wu_ref[...])
    u_ref[...] = u
    for gi, w in enumerate(windows):
        @pl.when(c == gi)
        def _(w=w):
            win = u
            for j in range(1, w):
                win = win + stp_ref[pool_buf - j]
            cnt = float(min(PAST_LEN + 1, w))
            pooled = (win / cnt - u).astype(BF16)
            pool_ref[...] = (_dot(pooled, wp_ref[...]) * sc_ref[...]).astype(pool_ref.dtype)

    v = _dot(x, wc_ref[...]) * _dot(x, wh_ref[...])
    v_ref[...] = v
    conv = cw_ref[0:1, :] * stc_ref[0] + cw_ref[1:2, :] * stc_ref[1] + cw_ref[2:3, :] * v
    conv_ref[...] = (_dot(x, wb_ref[...]) * conv).astype(conv_ref.dtype)


def _mixer_sample(x16, w_in16, stp_t, stc_t, conv_w, w_pool16, pool_scale):
    bs, d = x16.shape
    ngrp, pg, _ = w_pool16.shape
    pool_buf, conv_buf = stp_t.shape[0], stc_t.shape[0]
    pw, cw = stp_t.shape[2], stc_t.shape[2]
    assert cw == pw, "decode mixer walks pooling and conv column tiles together"
    kern = functools.partial(_mixer_sample_kernel, windows=POOL_WINDOWS, pool_buf=pool_buf)
    col = lambda off: pl.BlockSpec((d, pg), lambda c: (0, off + c))
    tile = pl.BlockSpec((bs, pg), lambda c: (0, c))
    return pl.pallas_call(
        kern,
        grid=(ngrp,),
        in_specs=[
            pl.BlockSpec((bs, d), lambda c: (0, 0)),
            col(0), col(pw // pg), col((pw + cw) // pg), col((pw + 2 * cw) // pg),
            pl.BlockSpec((pool_buf, bs, pg), lambda c: (0, 0, c)),
            pl.BlockSpec((conv_buf, bs, pg), lambda c: (0, 0, c)),
            pl.BlockSpec((3, pg), lambda c: (0, c)),
            pl.BlockSpec((None, pg, pg), lambda c: (c, 0, 0)),
            pl.BlockSpec((1, pg), lambda c: (0, c)),
        ],
        out_specs=[tile, tile, tile, tile],
        out_shape=[
            jax.ShapeDtypeStruct((bs, pw), BF16),
            jax.ShapeDtypeStruct((bs, cw), BF16),
            jax.ShapeDtypeStruct((bs, pw), F32),
            jax.ShapeDtypeStruct((bs, cw), F32),
        ],
        compiler_params=_params("arbitrary"),
        name="mixer_sample",
    )(x16, w_in16, w_in16, w_in16, w_in16, stp_t, stc_t, conv_w, w_pool16, pool_scale)


def _layer_norm(r, g, b):
    mu = jnp.mean(r, axis=-1, keepdims=True)
    d = r - mu
    var = jnp.mean(d * d, axis=-1, keepdims=True)
    return d * lax.rsqrt(var + LN_EPS) * g + b


def _outproj_kernel(pool_ref, conv_ref, wo_ref, x_ref, g_ref, b_ref, wr_ref,
                    h32_ref, h16_ref, hp_ref, sc_ref, acc, *, alpha, nk_pool):
    k = pl.program_id(1)
    nk = pl.num_programs(1)

    @pl.when(k == 0)
    def _():
        acc[...] = alpha * x_ref[...]

    @pl.when(k < nk_pool)
    def _():
        acc[...] += _dot(pool_ref[...], wo_ref[...])

    @pl.when(k >= nk_pool)
    def _():
        acc[...] += _dot(conv_ref[...], wo_ref[...])

    @pl.when(k == nk - 1)
    def _():
        h = _layer_norm(acc[...], g_ref[...], b_ref[...])
        h32_ref[...] = h
        h16 = h.astype(BF16)
        h16_ref[...] = h16
        hp_ref[...] = _pack_halves(h16)
        sc_ref[...] = jax.nn.sigmoid(_dot(h16, wr_ref[...]))


def _outproj(pool_all, conv_all, w_o16, x_all, ln_g, ln_b, w_r16, alpha):
    m, pw = pool_all.shape
    cw = conv_all.shape[1]
    d = w_o16.shape[1]
    e = w_r16.shape[1]
    tm = _pick_tile(m, 320, 16)
    tk = _pick_tile(math.gcd(pw, cw), 512, LANES)
    nkp, nkc = pw // tk, cw // tk
    kern = functools.partial(_outproj_kernel, alpha=alpha, nk_pool=nkp)
    row = lambda width: pl.BlockSpec((tm, width), lambda i, k: (i, 0))
    vec = pl.BlockSpec((1, d), lambda i, k: (0, 0))
    return pl.pallas_call(
        kern,
        grid=(m // tm, nkp + nkc),
        in_specs=[
            pl.BlockSpec((tm, tk), lambda i, k: (i, jnp.minimum(k, nkp - 1))),
            pl.BlockSpec((tm, tk), lambda i, k: (i, jnp.maximum(k - nkp, 0))),
            pl.BlockSpec((tk, d), lambda i, k: (k, 0)),
            row(d), vec, vec,
            pl.BlockSpec((d, e), lambda i, k: (0, 0)),
        ],
        out_specs=[row(d), row(d), row(d // 2), row(e)],
        out_shape=[
            jax.ShapeDtypeStruct((m, d), F32),
            jax.ShapeDtypeStruct((m, d), BF16),
            jax.ShapeDtypeStruct((m, d // 2), jnp.uint32),
            jax.ShapeDtypeStruct((m, e), F32),
        ],
        scratch_shapes=[pltpu.VMEM((tm, d), F32)],
        compiler_params=_params("arbitrary", "arbitrary"),
        name="outproj_ln_router",
    )(pool_all, conv_all, w_o16, x_all, ln_g, ln_b, w_r16)


def _slot_row(a, m_tok):
    return (a & (TOP_K - 1)) * m_tok + _token_of(a)


def _token_of(a):
    return lax.shift_right_logical(a, TOP_K_SHIFT)


def _moe_kernel(order_ref, e_ref, ep_ref, st_ref, cnt_ref,
                hp_hbm, wg_ref, wu_ref, wd_ref,
                y_hbm,
                xbuf, gu_acc, hb16, ybuf, gsem, ssem,
                *, ka, nb, tk, tn, de, m_tok):
    del e_ref, ep_ref
    i = pl.program_id(0)
    s = pl.program_id(1)
    ni = pl.num_programs(0)
    nsteps = ka + nb
    cnt = cnt_ref[i]
    slot = i % 2
    variant = (cnt + ROW_GRAIN - 1) // ROW_GRAIN
    rows_dyn = variant * ROW_GRAIN
    spare0 = TOP_K * m_tok

    nxt = jnp.minimum(i + 1, ni - 1)
    cnt_n, base_n = cnt_ref[nxt], st_ref[nxt]
    prev = jnp.maximum(i - 1, 0)
    cnt_p = jnp.where(i > 0, cnt_ref[prev], 0)
    base_p = st_ref[prev]

    def rounded(c):
        return (c + ROW_GRAIN - 1) // ROW_GRAIN * ROW_GRAIN

    def gather_copy(item_slot, j, tok):
        return pltpu.make_async_copy(hp_hbm.at[pl.ds(tok, 1), :], xbuf.at[item_slot, pl.ds(j, 1), :],
                                     gsem.at[item_slot])

    def scatter_copy(j, row):
        return pltpu.make_async_copy(ybuf.at[pl.ds(j, 1), :], y_hbm.at[pl.ds(row, 1), :], ssem.at[0])

    def gather_next(j):
        a = order_ref[base_n + jnp.maximum(jnp.minimum(j, cnt_n - 1), 0)]
        gather_copy(1 - slot, j, _token_of(a)).start()

    def scatter_prev(j):
        a = order_ref[base_p + jnp.maximum(jnp.minimum(j, cnt_p - 1), 0)]
        scatter_copy(j, jnp.where(j < cnt_p, _slot_row(a, m_tok), spare0 + j)).start()

    def issue_row_copies(rows):
        per = rows // nsteps
        for t in range(per):
            gather_next(s * per + t)

    def issue_scatter_copies(rows):
        per = rows // ka
        for t in range(per):
            scatter_prev(s * per + t)

    def loop_rows(lo, hi, fn):
        lax.fori_loop(lo, hi, lambda j, carry: (fn(j), carry)[1], 0)

    def wait_rows(count, block_copy):
        p = ITEM_ROWS
        while p >= 1:
            pl.when((count & p) != 0)(functools.partial(lambda q: block_copy(q).wait(), p))
            p //= 2

    def wait_gather(count, item_slot):
        wait_rows(count, lambda p: pltpu.make_async_copy(
            hp_hbm.at[pl.ds(0, p), :], xbuf.at[item_slot, pl.ds(0, p), :], gsem.at[item_slot]))

    def wait_scatter(count):
        wait_rows(count, lambda p: pltpu.make_async_copy(
            ybuf.at[pl.ds(0, p), :], y_hbm.at[pl.ds(0, p), :], ssem.at[0]))

    def for_variant(fn):
        for v in range(1, ITEM_ROWS // ROW_GRAIN + 1):
            pl.when(variant == v)(functools.partial(fn, v * ROW_GRAIN))

    @pl.when(s == 0)
    def _():
        @pl.when(i == 0)
        def _():
            xbuf[...] = jnp.zeros(xbuf.shape, xbuf.dtype)
            ybuf[...] = jnp.zeros(ybuf.shape, ybuf.dtype)
            loop_rows(0, cnt, lambda j: gather_copy(0, j, _token_of(order_ref[st_ref[0] + j])).start())

        issued = jnp.where(i == 0, cnt, jnp.maximum(rounded(cnt_p), cnt))
        wait_gather(issued, slot)

    @pl.when(s < ka)
    def _():
        chunks_per_half = ka // 2
        high = s >= chunks_per_half
        coff = pl.multiple_of((s % chunks_per_half) * tk, tk)

        def gate_up(rows):
            issue_row_copies(rows)
            issue_scatter_copies(rows)
            x16 = _unpack_half(xbuf[slot, 0:rows, pl.ds(coff, tk)], high)
            g = _dot(x16, wg_ref[...].astype(BF16))
            u = _dot(x16, wu_ref[...].astype(BF16))

            @pl.when(s == 0)
            def _():
                gu_acc[0:rows, 0:de] = g
                gu_acc[0:rows, de:2 * de] = u

            @pl.when(s > 0)
            def _():
                gu_acc[0:rows, 0:de] += g
                gu_acc[0:rows, de:2 * de] += u
        for_variant(gate_up)

        @pl.when(s == ka - 1)
        def _():
            loop_rows(rows_dyn, cnt_p, scatter_prev)

    @pl.when(s >= ka)
    def _():
        n = s - ka
        col0 = pl.multiple_of(n * tn, tn)

        @pl.when(n == 0)
        def _():
            wait_scatter(jnp.maximum(rows_dyn, cnt_p))

            def act(rows):
                hb16[0:rows, :] = (_silu(gu_acc[0:rows, 0:de]) * gu_acc[0:rows, de:2 * de]).astype(BF16)
            for_variant(act)

        def down(rows):
            issue_row_copies(rows)
            ybuf[0:rows, pl.ds(col0, tn)] = _dot(hb16[0:rows, :], wd_ref[...].astype(BF16))
        for_variant(down)

        @pl.when(n == nb - 1)
        def _():
            loop_rows(rows_dyn, cnt_n, gather_next)

            @pl.when(i == ni - 1)
            def _():
                loop_rows(0, cnt, lambda j: scatter_copy(
                    j, _slot_row(order_ref[st_ref[i] + j], m_tok)).start())
                wait_scatter(cnt)
                wait_gather(jnp.maximum(rows_dyn, cnt_n), 1 - slot)


def _moe_routed(hp, order, n_items, item_e, item_eprev, item_start, item_cnt, w_gate, w_up, w_down, layer):
    m_tok, dh = hp.shape
    d = 2 * dh
    de = w_gate.shape[-1]
    tk = _pick_tile(dh, min(1024, d // 4), LANES)
    tn = tk
    ka, nb = d // tk, d // tn
    assert de % LANES == 0 and nb >= 2 and ka % 2 == 0 and ROW_GRAIN % (ka + nb) == 0

    def wgu_map(i, s, order, e, ep, st, cnt):
        return (layer, e[i], jnp.minimum(s, ka - 1), 0)

    def wd_map(i, s, order, e, ep, st, cnt):
        first = s < ka
        return (layer, jnp.where(first, ep[i], e[i]), 0, jnp.where(first, nb - 1, s - ka))

    kern = functools.partial(_moe_kernel, ka=ka, nb=nb, tk=tk, tn=tn, de=de, m_tok=m_tok)
    grid_spec = pltpu.PrefetchScalarGridSpec(
        num_scalar_prefetch=5,
        grid=(n_items[0], ka + nb),
        in_specs=[
            pl.BlockSpec(memory_space=pl.ANY),
            pl.BlockSpec((None, None, tk, de), wgu_map),
            pl.BlockSpec((None, None, tk, de), wgu_map),
            pl.BlockSpec((None, None, de, tn), wd_map),
        ],
        out_specs=pl.BlockSpec(memory_space=pl.ANY),
        scratch_shapes=[
            pltpu.VMEM((2, ITEM_ROWS, dh), jnp.uint32),
            pltpu.VMEM((ITEM_ROWS, 2 * de), F32),
            pltpu.VMEM((ITEM_ROWS, de), BF16),
            pltpu.VMEM((ITEM_ROWS, d), F32),
            pltpu.SemaphoreType.DMA((2,)),
            pltpu.SemaphoreType.DMA((1,)),
        ],
    )
    return pl.pallas_call(
        kern,
        grid_spec=grid_spec,
        out_shape=jax.ShapeDtypeStruct((TOP_K * m_tok + ITEM_ROWS, d), F32),
        compiler_params=_params("arbitrary", "arbitrary"),
        name="moe_routed",
    )(order, item_e, item_eprev, item_start, item_cnt, hp, w_gate, w_up, w_down)


def _route(scores, bias, n_experts):
    m = scores.shape[0]
    a = m * TOP_K
    _, idx = lax.top_k(scores + bias[None, :].astype(F32), TOP_K)
    sel = jnp.take_along_axis(scores, idx, axis=1)
    gates = sel / jnp.sum(sel, axis=-1, keepdims=True) * ROUTED_SCALE

    abits = max(1, (a - 1).bit_length())
    assert (n_experts << abits) < 2 ** 31
    e_flat = idx.reshape(-1).astype(jnp.int32)
    keys = lax.sort((e_flat << abits) | jnp.arange(a, dtype=jnp.int32))
    order = keys & ((1 << abits) - 1)
    experts = jnp.arange(n_experts, dtype=jnp.int32)
    counts = jnp.sum((idx.astype(jnp.int32)[:, :, None] == experts[None, None, :]).astype(jnp.int32), axis=(0, 1))
    starts = jnp.cumsum(counts) - counts

    ni = n_experts + a // ITEM_ROWS
    n_items = (counts + ITEM_ROWS - 1) // ITEM_ROWS
    item_end = jnp.cumsum(n_items)
    total = item_end[-1]
    slots = jnp.arange(ni, dtype=jnp.int32)
    live = slots < total
    sl = jnp.minimum(slots, total - 1)
    ex = jnp.sum((item_end[None, :] <= sl[:, None]).astype(jnp.int32), axis=1)
    jj = sl - (item_end[ex] - n_items[ex])
    item_start = starts[ex] + jj * ITEM_ROWS
    item_cnt = jnp.where(live, jnp.minimum(ITEM_ROWS, counts[ex] - jj * ITEM_ROWS), 0)
    ex_prev = jnp.concatenate([ex[:1], ex[:-1]])
    return (gates, order.astype(jnp.int32), total.reshape(1).astype(jnp.int32), ex.astype(jnp.int32),
            ex_prev.astype(jnp.int32), item_start.astype(jnp.int32), item_cnt.astype(jnp.int32))


def _shared_kernel(h_ref, wgu_ref, wd_ref, out_ref, *, ch):
    c = pl.program_id(1)
    gu = _dot(h_ref[...], wgu_ref[...])
    hb = (_silu(gu[:, 0:ch]) * gu[:, ch:2 * ch]).astype(BF16)
    part = _dot(hb, wd_ref[...])

    @pl.when(c == 0)
    def _():
        out_ref[...] = part

    @pl.when(c > 0)
    def _():
        out_ref[...] += part


def _shared_ffn(h16, ws_gate16, ws_up16, ws_down16):
    m, d = h16.shape
    de = ws_gate16.shape[1]
    ch = _pick_tile(de, 384, LANES)
    nc = de // ch
    tm = _pick_tile(m, 320, 16)
    wgu = jnp.concatenate(
        [jnp.concatenate([ws_gate16[:, c * ch:(c + 1) * ch], ws_up16[:, c * ch:(c + 1) * ch]], axis=1)
         for c in range(nc)], axis=1)
    return pl.pallas_call(
        functools.partial(_shared_kernel, ch=ch),
        grid=(m // tm, nc),
        in_specs=[
            pl.BlockSpec((tm, d), lambda i, c: (i, 0)),
            pl.BlockSpec((d, 2 * ch), lambda i, c: (0, c)),
            pl.BlockSpec((ch, d), lambda i, c: (c, 0)),
        ],
        out_specs=pl.BlockSpec((tm, d), lambda i, c: (i, 0)),
        out_shape=jax.ShapeDtypeStruct((m, d), F32),
        compiler_params=_params("arbitrary", "arbitrary"),
        name="shared_ffn",
    )(h16, wgu, ws_down16)


def _combine_ln_kernel(h_ref, *refs, alpha, n_prompt):
    y_refs = refs[:TOP_K]
    gt_ref, s_ref, g_ref, b_ref, outp_ref, outs_ref = refs[TOP_K:]
    i = pl.program_id(0)
    routed = gt_ref[:, 0:1] * y_refs[0][...]
    for k in range(1, TOP_K):
        routed = routed + gt_ref[:, k:k + 1] * y_refs[k][...]
    y = _layer_norm(alpha * h_ref[...] + (routed + s_ref[...]), g_ref[...], b_ref[...])

    @pl.when(i < n_prompt)
    def _():
        outp_ref[...] = y

    @pl.when(i >= n_prompt)
    def _():
        outs_ref[...] = y


def _combine_ln(h32, y_slots, gates, shared, ln_g, ln_b, alpha, m_prompt):
    m, d = h32.shape
    m_sample = m - m_prompt
    tm = _pick_tile(math.gcd(m_prompt, m_sample), 64, SUBLANES)
    n_prompt = m_prompt // tm
    nt = m // tm
    row = pl.BlockSpec((tm, d), lambda i: (i, 0))
    vec = pl.BlockSpec((1, d), lambda i: (0, 0))
    slot_rows = [pl.BlockSpec((tm, d), functools.partial(lambda k, i: (k * nt + i, 0), k)) for k in range(TOP_K)]
    return pl.pallas_call(
        functools.partial(_combine_ln_kernel, alpha=alpha, n_prompt=n_prompt),
        grid=(nt,),
        in_specs=[row, *slot_rows, pl.BlockSpec((tm, TOP_K), lambda i: (i, 0)), row, vec, vec],
        out_specs=[pl.BlockSpec((tm, d), lambda i: (jnp.minimum(i, n_prompt - 1), 0)),
                   pl.BlockSpec((tm, d), lambda i: (jnp.maximum(i - n_prompt, 0), 0))],
        out_shape=[jax.ShapeDtypeStruct((m_prompt, d), F32), jax.ShapeDtypeStruct((m_sample, d), F32)],
        compiler_params=_params("arbitrary"),
        name="combine_ln",
    )(h32, *([y_slots] * TOP_K), gates, shared, ln_g, ln_b)


def _layer(xp, xs, st_pool, st_conv, alpha, layer, w_in, w_pool, pool_scale, conv_w, w_o,
           ln1_g, ln1_b, ln2_g, ln2_b, w_router, router_bias,
           w_gate, w_up, w_down, ws_gate, ws_up, ws_down):
    batch, seq, d = xp.shape
    bs = xs.shape[0]
    pool_buf, pw = st_pool.shape[1], st_pool.shape[2]
    conv_buf, cw = st_conv.shape[1], st_conv.shape[2]
    n_experts = w_router.shape[1]
    pg = w_pool.shape[1]

    xp2 = xp.reshape(batch * seq, d)
    xs2 = xs.reshape(bs, d)
    w_in16 = w_in.astype(BF16)
    w_pool16 = w_pool.astype(BF16)
    scale2 = pool_scale.reshape(1, pw)

    xp16 = xp2.astype(BF16)
    pool_p, new_pool_p = _pool_prompt(xp16, w_in16, w_pool16, scale2, batch, seq, pool_buf)
    conv_p, new_conv_p = _conv_prompt(xp16, w_in16, conv_w, batch, seq, pw, cw, pg, conv_buf)

    stp_t = jnp.transpose(st_pool, (1, 0, 2))
    stc_t = jnp.transpose(st_conv, (1, 0, 2))
    pool_s, conv_s, u_s, v_s = _mixer_sample(xs2.astype(BF16), w_in16, stp_t, stc_t, conv_w,
                                             w_pool16, scale2)
    new_pool_s = jnp.concatenate([st_pool[:, 1:], u_s[:, None, :]], axis=1)
    new_conv_s = jnp.concatenate([st_conv[:, 1:], v_s[:, None, :]], axis=1)

    x_all = jnp.concatenate([xp2, xs2], axis=0)
    pool_all = jnp.concatenate([pool_p, pool_s], axis=0)
    conv_all = jnp.concatenate([conv_p, conv_s], axis=0)
    h32, h16, hp, scores = _outproj(pool_all, conv_all, w_o.astype(BF16), x_all,
                                    ln1_g.reshape(1, d), ln1_b.reshape(1, d),
                                    w_router.astype(BF16), alpha)

    gates, order, n_items, item_e, item_ep, item_start, item_cnt = _route(scores, router_bias, n_experts)
    y_slots = _moe_routed(hp, order, n_items, item_e, item_ep, item_start, item_cnt,
                          w_gate, w_up, w_down, layer)
    shared = _shared_ffn(h16, ws_gate.astype(BF16), ws_up.astype(BF16), ws_down.astype(BF16))
    yp, ys = _combine_ln(h32, y_slots, gates, shared,
                         ln2_g.reshape(1, d), ln2_b.reshape(1, d), alpha, batch * seq)
    return (yp.reshape(batch, seq, d), ys.reshape(bs, xs.shape[1], d),
            new_pool_p, new_pool_s, new_conv_p, new_conv_s)


def kernel(x_prompt, x_sample, state_pool, state_conv, w_in, w_pool, pool_scale, conv_w, w_o,
           ln1_g, ln1_b, ln2_g, ln2_b, w_router, router_bias,
           w_gate, w_up, w_down, ws_gate, ws_up, ws_down):
    depth = w_in.shape[0]
    assert x_sample.shape[1] == 1, "decode rows carry one new token per sequence"
    alpha = (2.0 * depth) ** 0.25
    hp, hs = x_prompt, x_sample
    pool_p, pool_s, conv_p, conv_s = [], [], [], []
    for l in range(depth):
        hp, hs, npp, nps, ncp, ncs = _layer(
            hp, hs, state_pool[l], state_conv[l], alpha, l, w_in[l], w_pool[l], pool_scale[l],
            conv_w[l], w_o[l], ln1_g[l], ln1_b[l], ln2_g[l], ln2_b[l], w_router[l],
            router_bias[l], w_gate, w_up, w_down, ws_gate[l], ws_up[l], ws_down[l])
        pool_p.append(npp)
        pool_s.append(nps)
        conv_p.append(ncp)
        conv_s.append(ncs)
    return (hp, hs, jnp.stack(pool_p, 0), jnp.stack(pool_s, 0), jnp.stack(conv_p, 0), jnp.stack(conv_s, 0))
```

```python
import functools
import math

import jax
import jax.numpy as jnp
from jax import lax
from jax.experimental import pallas as pl
from jax.experimental.pallas import tpu as pltpu

POOL_WINDOWS = (2, 4, 8, 16)
TOP_K = 8
ROUTED_SCALE = 2.5
LN_EPS = 1e-5
PAST_LEN = 16384

LANES = 128
SUBLANES = 8
VMEM_LIMIT = 56 * 1024 * 1024

POOL_HALO = 32
CONV_HALO = 8
ITEM_ROWS = 512
ROW_GRAIN = 64
TOP_K_SHIFT = 3
assert 1 << TOP_K_SHIFT == TOP_K and ITEM_ROWS % ROW_GRAIN == 0

BF16 = jnp.bfloat16
F32 = jnp.float32


def _params(*sem):
    return pltpu.CompilerParams(dimension_semantics=sem, vmem_limit_bytes=VMEM_LIMIT)


def _pick_tile(n, target, mult):
    best = None
    for t in range(mult, min(n, target) + 1, mult):
        if n % t == 0:
            best = t
    assert best is not None, (n, target, mult)
    return best


def _dot(a, b):
    return jnp.dot(a, b, preferred_element_type=F32)


def _silu(x):
    return x * jax.nn.sigmoid(x)


def _pack_halves(x16):
    half = x16.shape[1] // 2
    return _pack_pair(x16[:, :half], x16[:, half:])


def _pack_pair(lo16, hi16):
    lo = lax.bitcast_convert_type(lo16.astype(F32), jnp.uint32)
    hi = lax.bitcast_convert_type(hi16.astype(F32), jnp.uint32)
    return (lo >> 16) | hi


def _unpack_half(words, high):
    bits = jnp.where(high, words & jnp.uint32(0xFFFF0000), words << 16)
    return lax.bitcast_convert_type(bits, F32).astype(BF16)


def _pool_prompt_kernel(x_ref, w_ref, wp_ref, sc_ref, out_ref, st_ref, u_buf, a_buf, b_buf,
                        *, tm, windows, pool_buf):
    g = pl.program_id(0)
    t = pl.program_id(2)
    nt = pl.num_programs(2)
    halo = POOL_HALO
    rows = halo + tm

    @pl.when(t == 0)
    def _():
        u_buf[0:halo, :] = jnp.zeros((halo, u_buf.shape[1]), F32)

    u = _dot(x_ref[...], w_ref[...])
    u_buf[halo:rows, :] = u
    pos1 = (t * tm + lax.broadcasted_iota(jnp.int32, (tm, 1), 0) + 1).astype(F32)

    for gi, w in enumerate(windows):
        @pl.when(g == gi)
        def _(w=w):
            src, shift, start = u_buf, 1, SUBLANES
            dsts = (a_buf, b_buf)
            lev = 0
            while shift < w:
                dst = dsts[lev % 2]
                n = rows - start
                dst[start:rows, :] = src[start:rows, :] + src[start - shift:start - shift + n, :]
                src, shift, start, lev = dst, shift * 2, start + SUBLANES, lev + 1
            win = src[halo:rows, :]
            inv_cnt = 1.0 / jnp.minimum(pos1, float(w))
            pooled = (win * inv_cnt - u).astype(BF16)
            out_ref[...] = (_dot(pooled, wp_ref[...]) * sc_ref[...]).astype(out_ref.dtype)

    @pl.when(t == nt - 1)
    def _():
        st_ref[...] = u_buf[rows - pool_buf:rows, :]

    u_buf[0:halo, :] = u_buf[tm:rows, :]


def _pool_prompt(x16, w_in16, w_pool16, pool_scale, batch, seq, pool_buf):
    m, d = x16.shape
    ngrp, pg, _ = w_pool16.shape
    pw = ngrp * pg
    assert max(POOL_WINDOWS) <= POOL_HALO and len(POOL_WINDOWS) == ngrp and pg % LANES == 0
    tm = _pick_tile(seq, 1024, 16)
    nt = seq // tm
    kern = functools.partial(_pool_prompt_kernel, tm=tm, windows=POOL_WINDOWS, pool_buf=pool_buf)
    return pl.pallas_call(
        kern,
        grid=(ngrp, batch, nt),
        in_specs=[
            pl.BlockSpec((tm, d), lambda g, b, t: (b * nt + t, 0)),
            pl.BlockSpec((d, pg), lambda g, b, t: (0, g)),
            pl.BlockSpec((None, pg, pg), lambda g, b, t: (g, 0, 0)),
            pl.BlockSpec((1, pg), lambda g, b, t: (0, g)),
        ],
        out_specs=[
            pl.BlockSpec((tm, pg), lambda g, b, t: (b * nt + t, g)),
            pl.BlockSpec((None, pool_buf, pg), lambda g, b, t: (b, 0, g)),
        ],
        out_shape=[
            jax.ShapeDtypeStruct((m, pw), BF16),
            jax.ShapeDtypeStruct((batch, pool_buf, pw), F32),
        ],
        scratch_shapes=[pltpu.VMEM((POOL_HALO + tm, pg), F32)] * 3,
        compiler_params=_params("arbitrary", "arbitrary", "arbitrary"),
        name="pool_prompt",
    )(x16, w_in16, w_pool16, pool_scale)


def _conv_prompt_kernel(x_ref, wb_ref, wc_ref, wh_ref, cw_ref, out_ref, st_ref, v_buf,
                        *, tm, conv_buf):
    t = pl.program_id(2)
    nt = pl.num_programs(2)
    halo = CONV_HALO
    rows = halo + tm

    @pl.when(t == 0)
    def _():
        v_buf[0:halo, :] = jnp.zeros((halo, v_buf.shape[1]), F32)

    x = x_ref[...]
    v = _dot(x, wc_ref[...]) * _dot(x, wh_ref[...])
    v_buf[halo:rows, :] = v
    conv = (cw_ref[0:1, :] * v_buf[halo - 2:rows - 2, :]
            + cw_ref[1:2, :] * v_buf[halo - 1:rows - 1, :]
            + cw_ref[2:3, :] * v)
    out_ref[...] = (_dot(x, wb_ref[...]) * conv).astype(out_ref.dtype)

    @pl.when(t == nt - 1)
    def _():
        st_ref[...] = v_buf[rows - conv_buf:rows, :]

    v_buf[0:halo, :] = v_buf[tm:rows, :]


def _conv_prompt(x16, w_in16, conv_w, batch, seq, pw, cw, ct, conv_buf):
    m, d = x16.shape
    assert conv_w.shape[0] == 3 and conv_buf == 2 and cw % ct == 0 and pw % ct == 0
    nc = cw // ct
    off_b, off_c, off_h = pw // ct, (pw + cw) // ct, (pw + 2 * cw) // ct
    tm = _pick_tile(seq, 1024, 16)
    nt = seq // tm
    kern = functools.partial(_conv_prompt_kernel, tm=tm, conv_buf=conv_buf)
    return pl.pallas_call(
        kern,
        grid=(nc, batch, nt),
        in_specs=[
            pl.BlockSpec((tm, d), lambda c, b, t: (b * nt + t, 0)),
            pl.BlockSpec((d, ct), lambda c, b, t: (0, off_b + c)),
            pl.BlockSpec((d, ct), lambda c, b, t: (0, off_c + c)),
            pl.BlockSpec((d, ct), lambda c, b, t: (0, off_h + c)),
            pl.BlockSpec((3, ct), lambda c, b, t: (0, c)),
        ],
        out_specs=[
            pl.BlockSpec((tm, ct), lambda c, b, t: (b * nt + t, c)),
            pl.BlockSpec((None, conv_buf, ct), lambda c, b, t: (b, 0, c)),
        ],
        out_shape=[
            jax.ShapeDtypeStruct((m, cw), BF16),
            jax.ShapeDtypeStruct((batch, conv_buf, cw), F32),
        ],
        scratch_shapes=[pltpu.VMEM((CONV_HALO + tm, ct), F32)],
        compiler_params=_params("arbitrary", "arbitrary", "arbitrary"),
        name="conv_prompt",
    )(x16, w_in16, w_in16, w_in16, conv_w)


def _mixer_sample_kernel(x_ref, wu_ref, wb_ref, wc_ref, wh_ref, stp_ref, stc_ref, cw_ref, wp_ref,
                         sc_ref, pool_ref, conv_ref, u_ref, v_ref, *, windows, pool_buf):
    c = pl.program_id(0)
    x = x_ref[...]
    u = _dot(x, wu_ref[...])
    u_ref[...] = u
    for gi, w in enumerate(windows):
        @pl.when(c == gi)
        def _(w=w):
            win = u
            for j in range(1, w):
                win = win + stp_ref[pool_buf - j]
            cnt = float(min(PAST_LEN + 1, w))
            pooled = (win / cnt - u).astype(BF16)
            pool_ref[...] = (_dot(pooled, wp_ref[...]) * sc_ref[...]).astype(pool_ref.dtype)

    v = _dot(x, wc_ref[...]) * _dot(x, wh_ref[...])
    v_ref[...] = v
    conv = cw_ref[0:1, :] * stc_ref[0] + cw_ref[1:2, :] * stc_ref[1] + cw_ref[2:3, :] * v
    conv_ref[...] = (_dot(x, wb_ref[...]) * conv).astype(conv_ref.dtype)


def _mixer_sample(x16, w_in16, stp_t, stc_t, conv_w, w_pool16, pool_scale):
    bs, d = x16.shape
    ngrp, pg, _ = w_pool16.shape
    pool_buf, conv_buf = stp_t.shape[0], stc_t.shape[0]
    pw, cw = stp_t.shape[2], stc_t.shape[2]
    assert cw == pw, "decode mixer walks pooling and conv column tiles together"
    kern = functools.partial(_mixer_sample_kernel, windows=POOL_WINDOWS, pool_buf=pool_buf)
    col = lambda off: pl.BlockSpec((d, pg), lambda c: (0, off + c))
    tile = pl.BlockSpec((bs, pg), lambda c: (0, c))
    return pl.pallas_call(
        kern,
        grid=(ngrp,),
        in_specs=[
            pl.BlockSpec((bs, d), lambda c: (0, 0)),
            col(0), col(pw // pg), col((pw + cw) // pg), col((pw + 2 * cw) // pg),
            pl.BlockSpec((pool_buf, bs, pg), lambda c: (0, 0, c)),
            pl.BlockSpec((conv_buf, bs, pg), lambda c: (0, 0, c)),
            pl.BlockSpec((3, pg), lambda c: (0, c)),
            pl.BlockSpec((None, pg, pg), lambda c: (c, 0, 0)),
            pl.BlockSpec((1, pg), lambda c: (0, c)),
        ],
        out_specs=[tile, tile, tile, tile],
        out_shape=[
            jax.ShapeDtypeStruct((bs, pw), BF16),
            jax.ShapeDtypeStruct((bs, cw), BF16),
            jax.ShapeDtypeStruct((bs, pw), F32),
            jax.ShapeDtypeStruct((bs, cw), F32),
        ],
        compiler_params=_params("arbitrary"),
        name="mixer_sample",
    )(x16, w_in16, w_in16, w_in16, w_in16, stp_t, stc_t, conv_w, w_pool16, pool_scale)


def _layer_norm(r, g, b):
    mu = jnp.mean(r, axis=-1, keepdims=True)
    d = r - mu
    var = jnp.mean(d * d, axis=-1, keepdims=True)
    return d * lax.rsqrt(var + LN_EPS) * g + b


def _outproj_kernel(pool_ref, conv_ref, wo_ref, x_ref, g_ref, b_ref, wr_ref,
                    h32_ref, h16_ref, hp_ref, sc_ref, acc, *, alpha, nk_pool):
    k = pl.program_id(1)
    nk = pl.num_programs(1)

    @pl.when(k == 0)
    def _():
        acc[...] = alpha * x_ref[...]

    @pl.when(k < nk_pool)
    def _():
        acc[...] += _dot(pool_ref[...], wo_ref[...])

    @pl.when(k >= nk_pool)
    def _():
        acc[...] += _dot(conv_ref[...], wo_ref[...])

    @pl.when(k == nk - 1)
    def _():
        h = _layer_norm(acc[...], g_ref[...], b_ref[...])
        h32_ref[...] = h
        h16 = h.astype(BF16)
        h16_ref[...] = h16
        hp_ref[...] = _pack_halves(h16)
        sc_ref[...] = jax.nn.sigmoid(_dot(h16, wr_ref[...]))


def _outproj(pool_all, conv_all, w_o16, x_all, ln_g, ln_b, w_r16, alpha):
    m, pw = pool_all.shape
    cw = conv_all.shape[1]
    d = w_o16.shape[1]
    e = w_r16.shape[1]
    tm = _pick_tile(m, 320, 16)
    tk = _pick_tile(math.gcd(pw, cw), 512, LANES)
    nkp, nkc = pw // tk, cw // tk
    kern = functools.partial(_outproj_kernel, alpha=alpha, nk_pool=nkp)
    row = lambda width: pl.BlockSpec((tm, width), lambda i, k: (i, 0))
    vec = pl.BlockSpec((1, d), lambda i, k: (0, 0))
    return pl.pallas_call(
        kern,
        grid=(m // tm, nkp + nkc),
        in_specs=[
            pl.BlockSpec((tm, tk), lambda i, k: (i, jnp.minimum(k, nkp - 1))),
            pl.BlockSpec((tm, tk), lambda i, k: (i, jnp.maximum(k - nkp, 0))),
            pl.BlockSpec((tk, d), lambda i, k: (k, 0)),
            row(d), vec, vec,
            pl.BlockSpec((d, e), lambda i, k: (0, 0)),
        ],
        out_specs=[row(d), row(d), row(d // 2), row(e)],
        out_shape=[
            jax.ShapeDtypeStruct((m, d), F32),
            jax.ShapeDtypeStruct((m, d), BF16),
            jax.ShapeDtypeStruct((m, d // 2), jnp.uint32),
            jax.ShapeDtypeStruct((m, e), F32),
        ],
        scratch_shapes=[pltpu.VMEM((tm, d), F32)],
        compiler_params=_params("arbitrary", "arbitrary"),
        name="outproj_ln_router",
    )(pool_all, conv_all, w_o16, x_all, ln_g, ln_b, w_r16)


def _slot_row(a, m_tok):
    return (a & (TOP_K - 1)) * m_tok + _token_of(a)


def _token_of(a):
    return lax.shift_right_logical(a, TOP_K_SHIFT)


def _moe_kernel(order_ref, e_ref, ep_ref, st_ref, cnt_ref,
                hp_hbm, wg_ref, wu_ref, wd_ref,
                y_hbm,
                xbuf, gu_acc, hb16, ylow, ypack, gsem, ssem,
                *, ka, nb, tk, tn, de, m_tok):
    del e_ref, ep_ref
    i = pl.program_id(0)
    s = pl.program_id(1)
    ni = pl.num_programs(0)
    cnt = cnt_ref[i]
    slot = i % 2
    variant = (cnt + ROW_GRAIN - 1) // ROW_GRAIN

    def gather_copy(item_slot, j, tok):
        return pltpu.make_async_copy(hp_hbm.at[pl.ds(tok, 1), :], xbuf.at[item_slot, pl.ds(j, 1), :],
                                     gsem.at[item_slot])

    def scatter_copy(j, row):
        return pltpu.make_async_copy(ypack.at[pl.ds(j, 1), :], y_hbm.at[pl.ds(row, 1), :], ssem.at[0])

    def share(count, part, parts):
        chunk = (count + parts - 1) // parts
        lo = jnp.minimum(part * chunk, count)
        return lo, jnp.minimum(lo + chunk, count)

    def start_gather(item, item_slot, lo, hi):
        base = st_ref[item]

        def body(j, carry):
            gather_copy(item_slot, j, _token_of(order_ref[base + j])).start()
            return carry
        lax.fori_loop(lo, hi, body, 0)

    def start_scatter(item, lo, hi):
        base = st_ref[item]

        def body(j, carry):
            scatter_copy(j, _slot_row(order_ref[base + j], m_tok)).start()
            return carry
        lax.fori_loop(lo, hi, body, 0)

    def wait_rows(count, block_copy):
        p = ITEM_ROWS
        while p >= 1:
            pl.when((count & p) != 0)(functools.partial(lambda q: block_copy(q).wait(), p))
            p //= 2

    def wait_gather(item, item_slot):
        wait_rows(cnt_ref[item], lambda p: pltpu.make_async_copy(
            hp_hbm.at[pl.ds(0, p), :], xbuf.at[item_slot, pl.ds(0, p), :], gsem.at[item_slot]))

    def wait_scatter(item):
        wait_rows(cnt_ref[item], lambda p: pltpu.make_async_copy(
            ypack.at[pl.ds(0, p), :], y_hbm.at[pl.ds(0, p), :], ssem.at[0]))

    def for_variant(fn):
        for v in range(1, ITEM_ROWS // ROW_GRAIN + 1):
            pl.when(variant == v)(functools.partial(fn, v * ROW_GRAIN))

    @pl.when(s == 0)
    def _():
        @pl.when(i == 0)
        def _():
            xbuf[...] = jnp.zeros(xbuf.shape, xbuf.dtype)
            start_gather(0, 0, 0, cnt)

        wait_gather(i, slot)

    @pl.when(i + 1 < ni)
    def _():
        lo, hi = share(cnt_ref[i + 1], s, ka + nb)
        start_gather(i + 1, 1 - slot, lo, hi)

    @pl.when(jnp.logical_and(s < ka, i > 0))
    def _():
        prev = jnp.maximum(i - 1, 0)
        lo, hi = share(cnt_ref[prev], s, ka)
        start_scatter(prev, lo, hi)

    @pl.when(s < ka)
    def _():
        chunks_per_half = ka // 2
        high = s >= chunks_per_half
        coff = pl.multiple_of((s % chunks_per_half) * tk, tk)

        def gate_up(rows):
            x16 = _unpack_half(xbuf[slot, 0:rows, pl.ds(coff, tk)], high)
            g = _dot(x16, wg_ref[...].astype(BF16))
            u = _dot(x16, wu_ref[...].astype(BF16))

            @pl.when(s == 0)
            def _():
                gu_acc[0:rows, 0:de] = g
                gu_acc[0:rows, de:2 * de] = u

            @pl.when(s > 0)
            def _():
                gu_acc[0:rows, 0:de] += g
                gu_acc[0:rows, de:2 * de] += u
        for_variant(gate_up)

    @pl.when(s >= ka)
    def _():
        n = s - ka
        half_blks = nb // 2
        woff = pl.multiple_of((n % half_blks) * tn, tn)

        @pl.when(n == 0)
        def _():
            @pl.when(i > 0)
            def _():
                wait_scatter(jnp.maximum(i - 1, 0))

            def act(rows):
                hb16[0:rows, :] = (_silu(gu_acc[0:rows, 0:de]) * gu_acc[0:rows, de:2 * de]).astype(BF16)
            for_variant(act)

        def down(rows):
            y = _dot(hb16[0:rows, :], wd_ref[...].astype(BF16))

            @pl.when(n < half_blks)
            def _():
                ylow[0:rows, pl.ds(woff, tn)] = y

            @pl.when(n >= half_blks)
            def _():
                ypack[0:rows, pl.ds(woff, tn)] = _pack_pair(ylow[0:rows, pl.ds(woff, tn)].astype(BF16),
                                                             y.astype(BF16))
        for_variant(down)

        @pl.when(jnp.logical_and(n == nb - 1, i == ni - 1))
        def _():
            start_scatter(i, 0, cnt)
            wait_scatter(i)


def _moe_routed(hp, order, n_items, item_e, item_eprev, item_start, item_cnt, w_gate, w_up, w_down, layer):
    m_tok, dh = hp.shape
    d = 2 * dh
    de = w_gate.shape[-1]
    tk = _pick_tile(dh, min(1024, d // 4), LANES)
    tn = tk
    ka, nb = d // tk, d // tn
    assert de % LANES == 0 and nb >= 2 and ka % 2 == 0 and nb % 2 == 0

    def wgu_map(i, s, order, e, ep, st, cnt):
        return (layer, e[i], jnp.minimum(s, ka - 1), 0)

    def wd_map(i, s, order, e, ep, st, cnt):
        first = s < ka
        return (layer, jnp.where(first, ep[i], e[i]), 0, jnp.where(first, nb - 1, s - ka))

    kern = functools.partial(_moe_kernel, ka=ka, nb=nb, tk=tk, tn=tn, de=de, m_tok=m_tok)
    grid_spec = pltpu.PrefetchScalarGridSpec(
        num_scalar_prefetch=5,
        grid=(n_items[0], ka + nb),
        in_specs=[
            pl.BlockSpec(memory_space=pl.ANY),
            pl.BlockSpec((None, None, tk, de), wgu_map),
            pl.BlockSpec((None, None, tk, de), wgu_map),
            pl.BlockSpec((None, None, de, tn), wd_map),
        ],
        out_specs=pl.BlockSpec(memory_space=pl.ANY),
        scratch_shapes=[
            pltpu.VMEM((2, ITEM_ROWS, dh), jnp.uint32),
            pltpu.VMEM((ITEM_ROWS, 2 * de), F32),
            pltpu.VMEM((ITEM_ROWS, de), BF16),
            pltpu.VMEM((ITEM_ROWS, dh), F32),
            pltpu.VMEM((ITEM_ROWS, dh), jnp.uint32),
            pltpu.SemaphoreType.DMA((2,)),
            pltpu.SemaphoreType.DMA((1,)),
        ],
    )
    return pl.pallas_call(
        kern,
        grid_spec=grid_spec,
        out_shape=jax.ShapeDtypeStruct((TOP_K * m_tok, dh), jnp.uint32),
        compiler_params=_params("arbitrary", "arbitrary"),
        name="moe_routed",
    )(order, item_e, item_eprev, item_start, item_cnt, hp, w_gate, w_up, w_down)


def _route(scores, bias, n_experts):
    m = scores.shape[0]
    a = m * TOP_K
    _, idx = lax.top_k(scores + bias[None, :].astype(F32), TOP_K)
    sel = jnp.take_along_axis(scores, idx, axis=1)
    gates = sel / jnp.sum(sel, axis=-1, keepdims=True) * ROUTED_SCALE

    abits = max(1, (a - 1).bit_length())
    assert (n_experts << abits) < 2 ** 31
    e_flat = idx.reshape(-1).astype(jnp.int32)
    keys = lax.sort((e_flat << abits) | jnp.arange(a, dtype=jnp.int32))
    order = keys & ((1 << abits) - 1)
    experts = jnp.arange(n_experts, dtype=jnp.int32)
    counts = jnp.sum((idx.astype(jnp.int32)[:, :, None] == experts[None, None, :]).astype(jnp.int32), axis=(0, 1))
    starts = jnp.cumsum(counts) - counts

    ni = n_experts + a // ITEM_ROWS
    n_items = (counts + ITEM_ROWS - 1) // ITEM_ROWS
    item_end = jnp.cumsum(n_items)
    total = item_end[-1]
    slots = jnp.arange(ni, dtype=jnp.int32)
    live = slots < total
    sl = jnp.minimum(slots, total - 1)
    ex = jnp.sum((item_end[None, :] <= sl[:, None]).astype(jnp.int32), axis=1)
    jj = sl - (item_end[ex] - n_items[ex])
    item_start = starts[ex] + jj * ITEM_ROWS
    item_cnt = jnp.where(live, jnp.minimum(ITEM_ROWS, counts[ex] - jj * ITEM_ROWS), 0)
    ex_prev = jnp.concatenate([ex[:1], ex[:-1]])
    return (gates, order.astype(jnp.int32), total.reshape(1).astype(jnp.int32), ex.astype(jnp.int32),
            ex_prev.astype(jnp.int32), item_start.astype(jnp.int32), item_cnt.astype(jnp.int32))


def _shared_kernel(h_ref, wgu_ref, wd_ref, out_ref, *, ch):
    c = pl.program_id(1)
    gu = _dot(h_ref[...], wgu_ref[...])
    hb = (_silu(gu[:, 0:ch]) * gu[:, ch:2 * ch]).astype(BF16)
    part = _dot(hb, wd_ref[...])

    @pl.when(c == 0)
    def _():
        out_ref[...] = part

    @pl.when(c > 0)
    def _():
        out_ref[...] += part


def _shared_ffn(h16, ws_gate16, ws_up16, ws_down16):
    m, d = h16.shape
    de = ws_gate16.shape[1]
    ch = _pick_tile(de, 384, LANES)
    nc = de // ch
    tm = _pick_tile(m, 320, 16)
    wgu = jnp.concatenate(
        [jnp.concatenate([ws_gate16[:, c * ch:(c + 1) * ch], ws_up16[:, c * ch:(c + 1) * ch]], axis=1)
         for c in range(nc)], axis=1)
    return pl.pallas_call(
        functools.partial(_shared_kernel, ch=ch),
        grid=(m // tm, nc),
        in_specs=[
            pl.BlockSpec((tm, d), lambda i, c: (i, 0)),
            pl.BlockSpec((d, 2 * ch), lambda i, c: (0, c)),
            pl.BlockSpec((ch, d), lambda i, c: (c, 0)),
        ],
        out_specs=pl.BlockSpec((tm, d), lambda i, c: (i, 0)),
        out_shape=jax.ShapeDtypeStruct((m, d), F32),
        compiler_params=_params("arbitrary", "arbitrary"),
        name="shared_ffn",
    )(h16, wgu, ws_down16)


def _combine_ln_kernel(h_ref, *refs, alpha, n_prompt):
    y_refs = refs[:TOP_K]
    gt_ref, s_ref, g_ref, b_ref, outp_ref, outs_ref = refs[TOP_K:]
    i = pl.program_id(0)
    lo = hi = None
    for k in range(TOP_K):
        words = y_refs[k][...]
        gate = gt_ref[:, k:k + 1]
        lo_k = gate * lax.bitcast_convert_type(words << 16, F32)
        hi_k = gate * lax.bitcast_convert_type(words & jnp.uint32(0xFFFF0000), F32)
        lo = lo_k if lo is None else lo + lo_k
        hi = hi_k if hi is None else hi + hi_k
    routed = jnp.concatenate([lo, hi], axis=1)
    y = _layer_norm(alpha * h_ref[...] + (routed + s_ref[...]), g_ref[...], b_ref[...])

    @pl.when(i < n_prompt)
    def _():
        outp_ref[...] = y

    @pl.when(i >= n_prompt)
    def _():
        outs_ref[...] = y


def _combine_ln(h32, y_slots, gates, shared, ln_g, ln_b, alpha, m_prompt):
    m, d = h32.shape
    m_sample = m - m_prompt
    tm = _pick_tile(math.gcd(m_prompt, m_sample), 128, SUBLANES)
    n_prompt = m_prompt // tm
    nt = m // tm
    row = pl.BlockSpec((tm, d), lambda i: (i, 0))
    vec = pl.BlockSpec((1, d), lambda i: (0, 0))
    slot_rows = [pl.BlockSpec((tm, d // 2), functools.partial(lambda k, i: (k * nt + i, 0), k))
                 for k in range(TOP_K)]
    return pl.pallas_call(
        functools.partial(_combine_ln_kernel, alpha=alpha, n_prompt=n_prompt),
        grid=(nt,),
        in_specs=[row, *slot_rows, pl.BlockSpec((tm, TOP_K), lambda i: (i, 0)), row, vec, vec],
        out_specs=[pl.BlockSpec((tm, d), lambda i: (jnp.minimum(i, n_prompt - 1), 0)),
                   pl.BlockSpec((tm, d), lambda i: (jnp.maximum(i - n_prompt, 0), 0))],
        out_shape=[jax.ShapeDtypeStruct((m_prompt, d), F32), jax.ShapeDtypeStruct((m_sample, d), F32)],
        compiler_params=_params("arbitrary"),
        name="combine_ln",
    )(h32, *([y_slots] * TOP_K), gates, shared, ln_g, ln_b)


def _layer(xp, xs, st_pool, st_conv, alpha, layer, w_in, w_pool, pool_scale, conv_w, w_o,
           ln1_g, ln1_b, ln2_g, ln2_b, w_router, router_bias,
           w_gate, w_up, w_down, ws_gate, ws_up, ws_down):
    batch, seq, d = xp.shape
    bs = xs.shape[0]
    pool_buf, pw = st_pool.shape[1], st_pool.shape[2]
    conv_buf, cw = st_conv.shape[1], st_conv.shape[2]
    n_experts = w_router.shape[1]
    pg = w_pool.shape[1]

    xp2 = xp.reshape(batch * seq, d)
    xs2 = xs.reshape(bs, d)
    w_in16 = w_in.astype(BF16)
    w_pool16 = w_pool.astype(BF16)
    scale2 = pool_scale.reshape(1, pw)

    xp16 = xp2.astype(BF16)
    pool_p, new_pool_p = _pool_prompt(xp16, w_in16, w_pool16, scale2, batch, seq, pool_buf)
    conv_p, new_conv_p = _conv_prompt(xp16, w_in16, conv_w, batch, seq, pw, cw, pg, conv_buf)

    stp_t = jnp.transpose(st_pool, (1, 0, 2))
    stc_t = jnp.transpose(st_conv, (1, 0, 2))
    pool_s, conv_s, u_s, v_s = _mixer_sample(xs2.astype(BF16), w_in16, stp_t, stc_t, conv_w,
                                             w_pool16, scale2)
    new_pool_s = jnp.concatenate([st_pool[:, 1:], u_s[:, None, :]], axis=1)
    new_conv_s = jnp.concatenate([st_conv[:, 1:], v_s[:, None, :]], axis=1)

    x_all = jnp.concatenate([xp2, xs2], axis=0)
    pool_all = jnp.concatenate([pool_p, pool_s], axis=0)
    conv_all = jnp.concatenate([conv_p, conv_s], axis=0)
    h32, h16, hp, scores = _outproj(pool_all, conv_all, w_o.astype(BF16), x_all,
                                    ln1_g.reshape(1, d), ln1_b.reshape(1, d),
                                    w_router.astype(BF16), alpha)

    gates, order, n_items, item_e, item_ep, item_start, item_cnt = _route(scores, router_bias, n_experts)
    y_slots = _moe_routed(hp, order, n_items, item_e, item_ep, item_start, item_cnt,
                          w_gate, w_up, w_down, layer)
    shared = _shared_ffn(h16, ws_gate.astype(BF16), ws_up.astype(BF16), ws_down.astype(BF16))
    yp, ys = _combine_ln(h32, y_slots, gates, shared,
                         ln2_g.reshape(1, d), ln2_b.reshape(1, d), alpha, batch * seq)
    return (yp.reshape(batch, seq, d), ys.reshape(bs, xs.shape[1], d),
            new_pool_p, new_pool_s, new_conv_p, new_conv_s)


def kernel(x_prompt, x_sample, state_pool, state_conv, w_in, w_pool, pool_scale, conv_w, w_o,
           ln1_g, ln1_b, ln2_g, ln2_b, w_router, router_bias,
           w_gate, w_up, w_down, ws_gate, ws_up, ws_down):
    depth = w_in.shape[0]
    assert x_sample.shape[1] == 1, "decode rows carry one new token per sequence"
    alpha = (2.0 * depth) ** 0.25
    hp, hs = x_prompt, x_sample
    pool_p, pool_s, conv_p, conv_s = [], [], [], []
    for l in range(depth):
        hp, hs, npp, nps, ncp, ncs = _layer(
            hp, hs, state_pool[l], state_conv[l], alpha, l, w_in[l], w_pool[l], pool_scale[l],
            conv_w[l], w_o[l], ln1_g[l], ln1_b[l], ln2_g[l], ln2_b[l], w_router[l],
            router_bias[l], w_gate, w_up, w_down, ws_gate[l], ws_up[l], ws_down[l])
        pool_p.append(npp)
        pool_s.append(nps)
        conv_p.append(ncp)
        conv_s.append(ncs)
    return (hp, hs, jnp.stack(pool_p, 0), jnp.stack(pool_s, 0), jnp.stack(conv_p, 0), jnp.stack(conv_s, 0))
```

```python
import functools
import math

import jax
import jax.numpy as jnp
from jax import lax
from jax.experimental import pallas as pl
from jax.experimental.pallas import tpu as pltpu

POOL_WINDOWS = (2, 4, 8, 16)
TOP_K = 8
ROUTED_SCALE = 2.5
LN_EPS = 1e-5
PAST_LEN = 16384

LANES = 128
SUBLANES = 8
VMEM_LIMIT = 56 * 1024 * 1024

POOL_HALO = 32
CONV_HALO = 8
ITEM_ROWS = 512
ROW_GRAIN = 64
TOP_K_SHIFT = 3
assert 1 << TOP_K_SHIFT == TOP_K and ITEM_ROWS % ROW_GRAIN == 0

BF16 = jnp.bfloat16
F32 = jnp.float32


def _params(*sem):
    return pltpu.CompilerParams(dimension_semantics=sem, vmem_limit_bytes=VMEM_LIMIT)


def _pick_tile(n, target, mult):
    best = None
    for t in range(mult, min(n, target) + 1, mult):
        if n % t == 0:
            best = t
    assert best is not None, (n, target, mult)
    return best


def _dot(a, b):
    return jnp.dot(a, b, preferred_element_type=F32)


def _silu(x):
    return x * jax.nn.sigmoid(x)


def _pack_halves(x16):
    half = x16.shape[1] // 2
    return _pack_pair(x16[:, :half], x16[:, half:])


def _pack_pair(lo16, hi16):
    lo = lax.bitcast_convert_type(lo16.astype(F32), jnp.uint32)
    hi = lax.bitcast_convert_type(hi16.astype(F32), jnp.uint32)
    return (lo >> 16) | hi


def _unpack_half(words, high):
    bits = jnp.where(high, words & jnp.uint32(0xFFFF0000), words << 16)
    return lax.bitcast_convert_type(bits, F32).astype(BF16)


def _pool_prompt_kernel(x_ref, w_ref, wp_ref, sc_ref, out_ref, st_ref, u_buf, a_buf, b_buf,
                        *, tm, windows, pool_buf):
    g = pl.program_id(0)
    t = pl.program_id(2)
    nt = pl.num_programs(2)
    halo = POOL_HALO
    rows = halo + tm

    @pl.when(t == 0)
    def _():
        u_buf[0:halo, :] = jnp.zeros((halo, u_buf.shape[1]), F32)

    u = _dot(x_ref[...], w_ref[...])
    u_buf[halo:rows, :] = u
    pos1 = (t * tm + lax.broadcasted_iota(jnp.int32, (tm, 1), 0) + 1).astype(F32)

    for gi, w in enumerate(windows):
        @pl.when(g == gi)
        def _(w=w):
            src, shift, start = u_buf, 1, SUBLANES
            dsts = (a_buf, b_buf)
            lev = 0
            while shift < w:
                dst = dsts[lev % 2]
                n = rows - start
                dst[start:rows, :] = src[start:rows, :] + src[start - shift:start - shift + n, :]
                src, shift, start, lev = dst, shift * 2, start + SUBLANES, lev + 1
            win = src[halo:rows, :]
            inv_cnt = 1.0 / jnp.minimum(pos1, float(w))
            pooled = (win * inv_cnt - u).astype(BF16)
            out_ref[...] = (_dot(pooled, wp_ref[...]) * sc_ref[...]).astype(out_ref.dtype)

    @pl.when(t == nt - 1)
    def _():
        st_ref[...] = u_buf[rows - pool_buf:rows, :]

    u_buf[0:halo, :] = u_buf[tm:rows, :]


def _pool_prompt(x16, w_in16, w_pool16, pool_scale, batch, seq, pool_buf):
    m, d = x16.shape
    ngrp, pg, _ = w_pool16.shape
    pw = ngrp * pg
    assert max(POOL_WINDOWS) <= POOL_HALO and len(POOL_WINDOWS) == ngrp and pg % LANES == 0
    tm = _pick_tile(seq, 1024, 16)
    nt = seq // tm
    kern = functools.partial(_pool_prompt_kernel, tm=tm, windows=POOL_WINDOWS, pool_buf=pool_buf)
    return pl.pallas_call(
        kern,
        grid=(ngrp, batch, nt),
        in_specs=[
            pl.BlockSpec((tm, d), lambda g, b, t: (b * nt + t, 0)),
            pl.BlockSpec((d, pg), lambda g, b, t: (0, g)),
            pl.BlockSpec((None, pg, pg), lambda g, b, t: (g, 0, 0)),
            pl.BlockSpec((1, pg), lambda g, b, t: (0, g)),
        ],
        out_specs=[
            pl.BlockSpec((tm, pg), lambda g, b, t: (b * nt + t, g)),
            pl.BlockSpec((None, pool_buf, pg), lambda g, b, t: (b, 0, g)),
        ],
        out_shape=[
            jax.ShapeDtypeStruct((m, pw), BF16),
            jax.ShapeDtypeStruct((batch, pool_buf, pw), F32),
        ],
        scratch_shapes=[pltpu.VMEM((POOL_HALO + tm, pg), F32)] * 3,
        compiler_params=_params("arbitrary", "arbitrary", "arbitrary"),
        name="pool_prompt",
    )(x16, w_in16, w_pool16, pool_scale)


def _conv_prompt_kernel(x_ref, wb_ref, wc_ref, wh_ref, cw_ref, out_ref, st_ref, v_buf,
                        *, tm, conv_buf):
    t = pl.program_id(2)
    nt = pl.num_programs(2)
    halo = CONV_HALO
    rows = halo + tm

    @pl.when(t == 0)
    def _():
        v_buf[0:halo, :] = jnp.zeros((halo, v_buf.shape[1]), F32)

    x = x_ref[...]
    v = _dot(x, wc_ref[...]) * _dot(x, wh_ref[...])
    v_buf[halo:rows, :] = v
    conv = (cw_ref[0:1, :] * v_buf[halo - 2:rows - 2, :]
            + cw_ref[1:2, :] * v_buf[halo - 1:rows - 1, :]
            + cw_ref[2:3, :] * v)
    out_ref[...] = (_dot(x, wb_ref[...]) * conv).astype(out_ref.dtype)

    @pl.when(t == nt - 1)
    def _():
        st_ref[...] = v_buf[rows - conv_buf:rows, :]

    v_buf[0:halo, :] = v_buf[tm:rows, :]


def _conv_prompt(x16, w_in16, conv_w, batch, seq, pw, cw, ct, conv_buf):
    m, d = x16.shape
    assert conv_w.shape[0] == 3 and conv_buf == 2 and cw % ct == 0 and pw % ct == 0
    nc = cw // ct
    off_b, off_c, off_h = pw // ct, (pw + cw) // ct, (pw + 2 * cw) // ct
    tm = _pick_tile(seq, 1024, 16)
    nt = seq // tm
    kern = functools.partial(_conv_prompt_kernel, tm=tm, conv_buf=conv_buf)
    return pl.pallas_call(
        kern,
        grid=(nc, batch, nt),
        in_specs=[
            pl.BlockSpec((tm, d), lambda c, b, t: (b * nt + t, 0)),
            pl.BlockSpec((d, ct), lambda c, b, t: (0, off_b + c)),
            pl.BlockSpec((d, ct), lambda c, b, t: (0, off_c + c)),
            pl.BlockSpec((d, ct), lambda c, b, t: (0, off_h + c)),
            pl.BlockSpec((3, ct), lambda c, b, t: (0, c)),
        ],
        out_specs=[
            pl.BlockSpec((tm, ct), lambda c, b, t: (b * nt + t, c)),
            pl.BlockSpec((None, conv_buf, ct), lambda c, b, t: (b, 0, c)),
        ],
        out_shape=[
            jax.ShapeDtypeStruct((m, cw), BF16),
            jax.ShapeDtypeStruct((batch, conv_buf, cw), F32),
        ],
        scratch_shapes=[pltpu.VMEM((CONV_HALO + tm, ct), F32)],
        compiler_params=_params("arbitrary", "arbitrary", "arbitrary"),
        name="conv_prompt",
    )(x16, w_in16, w_in16, w_in16, conv_w)


def _mixer_sample_kernel(x_ref, wu_ref, wb_ref, wc_ref, wh_ref, stp_ref, stc_ref, cw_ref, wp_ref,
                         sc_ref, pool_ref, conv_ref, u_ref, v_ref, *, windows, pool_buf):
    c = pl.program_id(0)
    x = x_ref[...]
    u = _dot(x, wu_ref[...])
    u_ref[...] = u
    for gi, w in enumerate(windows):
        @pl.when(c == gi)
        def _(w=w):
            win = u
            for j in range(1, w):
                win = win + stp_ref[pool_buf - j]
            cnt = float(min(PAST_LEN + 1, w))
            pooled = (win / cnt - u).astype(BF16)
            pool_ref[...] = (_dot(pooled, wp_ref[...]) * sc_ref[...]).astype(pool_ref.dtype)

    v = _dot(x, wc_ref[...]) * _dot(x, wh_ref[...])
    v_ref[...] = v
    conv = cw_ref[0:1, :] * stc_ref[0] + cw_ref[1:2, :] * stc_ref[1] + cw_ref[2:3, :] * v
    conv_ref[...] = (_dot(x, wb_ref[...]) * conv).astype(conv_ref.dtype)


def _mixer_sample(x16, w_in16, stp_t, stc_t, conv_w, w_pool16, pool_scale):
    bs, d = x16.shape
    ngrp, pg, _ = w_pool16.shape
    pool_buf, conv_buf = stp_t.shape[0], stc_t.shape[0]
    pw, cw = stp_t.shape[2], stc_t.shape[2]
    assert cw == pw, "decode mixer walks pooling and conv column tiles together"
    kern = functools.partial(_mixer_sample_kernel, windows=POOL_WINDOWS, pool_buf=pool_buf)
    col = lambda off: pl.BlockSpec((d, pg), lambda c: (0, off + c))
    tile = pl.BlockSpec((bs, pg), lambda c: (0, c))
    return pl.pallas_call(
        kern,
        grid=(ngrp,),
        in_specs=[
            pl.BlockSpec((bs, d), lambda c: (0, 0)),
            col(0), col(pw // pg), col((pw + cw) // pg), col((pw + 2 * cw) // pg),
            pl.BlockSpec((pool_buf, bs, pg), lambda c: (0, 0, c)),
            pl.BlockSpec((conv_buf, bs, pg), lambda c: (0, 0, c)),
            pl.BlockSpec((3, pg), lambda c: (0, c)),
            pl.BlockSpec((None, pg, pg), lambda c: (c, 0, 0)),
            pl.BlockSpec((1, pg), lambda c: (0, c)),
        ],
        out_specs=[tile, tile, tile, tile],
        out_shape=[
            jax.ShapeDtypeStruct((bs, pw), BF16),
            jax.ShapeDtypeStruct((bs, cw), BF16),
            jax.ShapeDtypeStruct((bs, pw), F32),
            jax.ShapeDtypeStruct((bs, cw), F32),
        ],
        compiler_params=_params("arbitrary"),
        name="mixer_sample",
    )(x16, w_in16, w_in16, w_in16, w_in16, stp_t, stc_t, conv_w, w_pool16, pool_scale)


def _layer_norm(r, g, b):
    mu = jnp.mean(r, axis=-1, keepdims=True)
    d = r - mu
    var = jnp.mean(d * d, axis=-1, keepdims=True)
    return d * lax.rsqrt(var + LN_EPS) * g + b


def _outproj_kernel(pool_ref, conv_ref, wo_ref, x_ref, g_ref, b_ref, wr_ref,
                    h32_ref, h16_ref, hp_ref, sc_ref, acc, *, alpha, nk_pool):
    k = pl.program_id(1)
    nk = pl.num_programs(1)

    @pl.when(k == 0)
    def _():
        acc[...] = _dot(pool_ref[...], wo_ref[...])

    @pl.when(jnp.logical_and(k > 0, k < nk_pool))
    def _():
        acc[...] += _dot(pool_ref[...], wo_ref[...])

    @pl.when(k >= nk_pool)
    def _():
        acc[...] += _dot(conv_ref[...], wo_ref[...])

    xc = x_ref.shape[1]
    cols = pl.ds(pl.multiple_of(k * xc, xc), xc)
    acc[:, cols] += alpha * x_ref[...]

    @pl.when(k == nk - 1)
    def _():
        h = _layer_norm(acc[...], g_ref[...], b_ref[...])
        h32_ref[...] = h
        h16 = h.astype(BF16)
        h16_ref[...] = h16
        hp_ref[...] = _pack_halves(h16)
        sc_ref[...] = jax.nn.sigmoid(_dot(h16, wr_ref[...]))


def _outproj(pool_all, conv_all, w_o16, x_all, ln_g, ln_b, w_r16, alpha):
    m, pw = pool_all.shape
    cw = conv_all.shape[1]
    d = w_o16.shape[1]
    e = w_r16.shape[1]
    tm = _pick_tile(m, 416, 16)
    tk = _pick_tile(math.gcd(pw, cw), 512, LANES)
    nkp, nkc = pw // tk, cw // tk
    xc = d // (nkp + nkc)
    assert xc % LANES == 0
    kern = functools.partial(_outproj_kernel, alpha=alpha, nk_pool=nkp)
    row = lambda width: pl.BlockSpec((tm, width), lambda i, k: (i, 0))
    vec = pl.BlockSpec((1, d), lambda i, k: (0, 0))
    return pl.pallas_call(
        kern,
        grid=(m // tm, nkp + nkc),
        in_specs=[
            pl.BlockSpec((tm, tk), lambda i, k: (i, jnp.minimum(k, nkp - 1))),
            pl.BlockSpec((tm, tk), lambda i, k: (i, jnp.maximum(k - nkp, 0))),
            pl.BlockSpec((tk, d), lambda i, k: (k, 0)),
            pl.BlockSpec((tm, xc), lambda i, k: (i, k)), vec, vec,
            pl.BlockSpec((d, e), lambda i, k: (0, 0)),
        ],
        out_specs=[row(d), row(d), row(d // 2), row(e)],
        out_shape=[
            jax.ShapeDtypeStruct((m, d), F32),
            jax.ShapeDtypeStruct((m, d), BF16),
            jax.ShapeDtypeStruct((m, d // 2), jnp.uint32),
            jax.ShapeDtypeStruct((m, e), F32),
        ],
        scratch_shapes=[pltpu.VMEM((tm, d), F32)],
        compiler_params=_params("arbitrary", "arbitrary"),
        name="outproj_ln_router",
    )(pool_all, conv_all, w_o16, x_all, ln_g, ln_b, w_r16)


def _slot_row(a, m_tok):
    return (a & (TOP_K - 1)) * m_tok + _token_of(a)


def _token_of(a):
    return lax.shift_right_logical(a, TOP_K_SHIFT)


def _moe_kernel(order_ref, e_ref, ep_ref, st_ref, cnt_ref,
                hp_hbm, wg_ref, wu_ref, wd_ref,
                y_hbm,
                xbuf, gu_acc, hb16, ylow, ypack, gsem, ssem,
                *, ka, nb, tk, tn, de, m_tok):
    del e_ref, ep_ref
    i = pl.program_id(0)
    s = pl.program_id(1)
    ni = pl.num_programs(0)
    cnt = cnt_ref[i]
    slot = i % 2
    variant = (cnt + ROW_GRAIN - 1) // ROW_GRAIN

    def gather_copy(item_slot, j, tok):
        return pltpu.make_async_copy(hp_hbm.at[pl.ds(tok, 1), :], xbuf.at[item_slot, pl.ds(j, 1), :],
                                     gsem.at[item_slot])

    def scatter_copy(j, row):
        return pltpu.make_async_copy(ypack.at[pl.ds(j, 1), :], y_hbm.at[pl.ds(row, 1), :], ssem.at[0])

    def share(count, part, parts):
        chunk = (count + parts - 1) // parts
        lo = jnp.minimum(part * chunk, count)
        return lo, jnp.minimum(lo + chunk, count)

    def start_gather(item, item_slot, lo, hi):
        base = st_ref[item]

        def body(j, carry):
            gather_copy(item_slot, j, _token_of(order_ref[base + j])).start()
            return carry
        lax.fori_loop(lo, hi, body, 0)

    def start_scatter(item, lo, hi):
        base = st_ref[item]

        def body(j, carry):
            scatter_copy(j, _slot_row(order_ref[base + j], m_tok)).start()
            return carry
        lax.fori_loop(lo, hi, body, 0)

    def wait_rows(count, block_copy):
        p = ITEM_ROWS
        while p >= 1:
            pl.when((count & p) != 0)(functools.partial(lambda q: block_copy(q).wait(), p))
            p //= 2

    def wait_gather(item, item_slot):
        wait_rows(cnt_ref[item], lambda p: pltpu.make_async_copy(
            hp_hbm.at[pl.ds(0, p), :], xbuf.at[item_slot, pl.ds(0, p), :], gsem.at[item_slot]))

    def wait_scatter(item):
        wait_rows(cnt_ref[item], lambda p: pltpu.make_async_copy(
            ypack.at[pl.ds(0, p), :], y_hbm.at[pl.ds(0, p), :], ssem.at[0]))

    def for_variant(fn):
        for v in range(1, ITEM_ROWS // ROW_GRAIN + 1):
            pl.when(variant == v)(functools.partial(fn, v * ROW_GRAIN))

    @pl.when(s == 0)
    def _():
        @pl.when(i == 0)
        def _():
            xbuf[...] = jnp.zeros(xbuf.shape, xbuf.dtype)
            start_gather(0, 0, 0, cnt)

        wait_gather(i, slot)

    @pl.when(i + 1 < ni)
    def _():
        lo, hi = share(cnt_ref[i + 1], s, ka + nb)
        start_gather(i + 1, 1 - slot, lo, hi)

    @pl.when(jnp.logical_and(s < ka, i > 0))
    def _():
        prev = jnp.maximum(i - 1, 0)
        lo, hi = share(cnt_ref[prev], s, ka)
        start_scatter(prev, lo, hi)

    @pl.when(s < ka)
    def _():
        chunks_per_half = ka // 2
        high = s >= chunks_per_half
        coff = pl.multiple_of((s % chunks_per_half) * tk, tk)

        def gate_up(rows):
            x16 = _unpack_half(xbuf[slot, 0:rows, pl.ds(coff, tk)], high)
            g = _dot(x16, wg_ref[...].astype(BF16))
            u = _dot(x16, wu_ref[...].astype(BF16))

            @pl.when(s == 0)
            def _():
                gu_acc[0:rows, 0:de] = g
                gu_acc[0:rows, de:2 * de] = u

            @pl.when(s > 0)
            def _():
                gu_acc[0:rows, 0:de] += g
                gu_acc[0:rows, de:2 * de] += u
        for_variant(gate_up)

    @pl.when(s >= ka)
    def _():
        n = s - ka
        half_blks = nb // 2
        woff = pl.multiple_of((n % half_blks) * tn, tn)

        @pl.when(n == 0)
        def _():
            @pl.when(i > 0)
            def _():
                wait_scatter(jnp.maximum(i - 1, 0))

            def act(rows):
                hb16[0:rows, :] = (_silu(gu_acc[0:rows, 0:de]) * gu_acc[0:rows, de:2 * de]).astype(BF16)
            for_variant(act)

        def down(rows):
            y = _dot(hb16[0:rows, :], wd_ref[...].astype(BF16))

            @pl.when(n < half_blks)
            def _():
                ylow[0:rows, pl.ds(woff, tn)] = y

            @pl.when(n >= half_blks)
            def _():
                ypack[0:rows, pl.ds(woff, tn)] = _pack_pair(ylow[0:rows, pl.ds(woff, tn)].astype(BF16),
                                                             y.astype(BF16))
        for_variant(down)

        @pl.when(jnp.logical_and(n == nb - 1, i == ni - 1))
        def _():
            start_scatter(i, 0, cnt)
            wait_scatter(i)


def _moe_routed(hp, order, n_items, item_e, item_eprev, item_start, item_cnt, w_gate, w_up, w_down, layer):
    m_tok, dh = hp.shape
    d = 2 * dh
    de = w_gate.shape[-1]
    tk = _pick_tile(dh, min(1024, d // 4), LANES)
    tn = tk
    ka, nb = d // tk, d // tn
    assert de % LANES == 0 and nb >= 2 and ka % 2 == 0 and nb % 2 == 0

    def wgu_map(i, s, order, e, ep, st, cnt):
        return (layer, e[i], jnp.minimum(s, ka - 1), 0)

    def wd_map(i, s, order, e, ep, st, cnt):
        first = s < ka
        return (layer, jnp.where(first, ep[i], e[i]), 0, jnp.where(first, nb - 1, s - ka))

    kern = functools.partial(_moe_kernel, ka=ka, nb=nb, tk=tk, tn=tn, de=de, m_tok=m_tok)
    grid_spec = pltpu.PrefetchScalarGridSpec(
        num_scalar_prefetch=5,
        grid=(n_items[0], ka + nb),
        in_specs=[
            pl.BlockSpec(memory_space=pl.ANY),
            pl.BlockSpec((None, None, tk, de), wgu_map),
            pl.BlockSpec((None, None, tk, de), wgu_map),
            pl.BlockSpec((None, None, de, tn), wd_map),
        ],
        out_specs=pl.BlockSpec(memory_space=pl.ANY),
        scratch_shapes=[
            pltpu.VMEM((2, ITEM_ROWS, dh), jnp.uint32),
            pltpu.VMEM((ITEM_ROWS, 2 * de), F32),
            pltpu.VMEM((ITEM_ROWS, de), BF16),
            pltpu.VMEM((ITEM_ROWS, dh), F32),
            pltpu.VMEM((ITEM_ROWS, dh), jnp.uint32),
            pltpu.SemaphoreType.DMA((2,)),
            pltpu.SemaphoreType.DMA((1,)),
        ],
    )
    return pl.pallas_call(
        kern,
        grid_spec=grid_spec,
        out_shape=jax.ShapeDtypeStruct((TOP_K * m_tok, dh), jnp.uint32),
        compiler_params=_params("arbitrary", "arbitrary"),
        name="moe_routed",
    )(order, item_e, item_eprev, item_start, item_cnt, hp, w_gate, w_up, w_down)


def _route(scores, bias, n_experts):
    m = scores.shape[0]
    a = m * TOP_K
    _, idx = lax.top_k(scores + bias[None, :].astype(F32), TOP_K)
    sel = jnp.take_along_axis(scores, idx, axis=1)
    gates = sel / jnp.sum(sel, axis=-1, keepdims=True) * ROUTED_SCALE

    abits = max(1, (a - 1).bit_length())
    assert (n_experts << abits) < 2 ** 31
    e_flat = idx.reshape(-1).astype(jnp.int32)
    keys = lax.sort((e_flat << abits) | jnp.arange(a, dtype=jnp.int32))
    order = keys & ((1 << abits) - 1)
    experts = jnp.arange(n_experts, dtype=jnp.int32)
    counts = jnp.sum((idx.astype(jnp.int32)[:, :, None] == experts[None, None, :]).astype(jnp.int32), axis=(0, 1))
    starts = jnp.cumsum(counts) - counts

    ni = n_experts + a // ITEM_ROWS
    n_items = (counts + ITEM_ROWS - 1) // ITEM_ROWS
    item_end = jnp.cumsum(n_items)
    total = item_end[-1]
    slots = jnp.arange(ni, dtype=jnp.int32)
    live = slots < total
    sl = jnp.minimum(slots, total - 1)
    ex = jnp.sum((item_end[None, :] <= sl[:, None]).astype(jnp.int32), axis=1)
    jj = sl - (item_end[ex] - n_items[ex])
    item_start = starts[ex] + jj * ITEM_ROWS
    item_cnt = jnp.where(live, jnp.minimum(ITEM_ROWS, counts[ex] - jj * ITEM_ROWS), 0)
    ex_prev = jnp.concatenate([ex[:1], ex[:-1]])
    return (gates, order.astype(jnp.int32), total.reshape(1).astype(jnp.int32), ex.astype(jnp.int32),
            ex_prev.astype(jnp.int32), item_start.astype(jnp.int32), item_cnt.astype(jnp.int32))


def _shared_kernel(h_ref, wgu_ref, wd_ref, out_ref, *, ch):
    c = pl.program_id(1)
    gu = _dot(h_ref[...], wgu_ref[...])
    hb = (_silu(gu[:, 0:ch]) * gu[:, ch:2 * ch]).astype(BF16)
    part = _dot(hb, wd_ref[...])

    @pl.when(c == 0)
    def _():
        out_ref[...] = part

    @pl.when(c > 0)
    def _():
        out_ref[...] += part


def _shared_ffn(h16, ws_gate16, ws_up16, ws_down16):
    m, d = h16.shape
    de = ws_gate16.shape[1]
    ch = _pick_tile(de, 384, LANES)
    nc = de // ch
    tm = _pick_tile(m, 416, 16)
    wgu = jnp.concatenate(
        [jnp.concatenate([ws_gate16[:, c * ch:(c + 1) * ch], ws_up16[:, c * ch:(c + 1) * ch]], axis=1)
         for c in range(nc)], axis=1)
    return pl.pallas_call(
        functools.partial(_shared_kernel, ch=ch),
        grid=(m // tm, nc),
        in_specs=[
            pl.BlockSpec((tm, d), lambda i, c: (i, 0)),
            pl.BlockSpec((d, 2 * ch), lambda i, c: (0, c)),
            pl.BlockSpec((ch, d), lambda i, c: (c, 0)),
        ],
        out_specs=pl.BlockSpec((tm, d), lambda i, c: (i, 0)),
        out_shape=jax.ShapeDtypeStruct((m, d), F32),
        compiler_params=_params("arbitrary", "arbitrary"),
        name="shared_ffn",
    )(h16, wgu, ws_down16)


def _combine_ln_kernel(h_ref, *refs, alpha, n_prompt):
    y_refs = refs[:TOP_K]
    gt_ref, s_ref, g_ref, b_ref, outp_ref, outs_ref = refs[TOP_K:]
    i = pl.program_id(0)
    lo = hi = None
    for k in range(TOP_K):
        words = y_refs[k][...]
        gate = gt_ref[:, k:k + 1]
        lo_k = gate * lax.bitcast_convert_type(words << 16, F32)
        hi_k = gate * lax.bitcast_convert_type(words & jnp.uint32(0xFFFF0000), F32)
        lo = lo_k if lo is None else lo + lo_k
        hi = hi_k if hi is None else hi + hi_k
    routed = jnp.concatenate([lo, hi], axis=1)
    y = _layer_norm(alpha * h_ref[...] + (routed + s_ref[...]), g_ref[...], b_ref[...])

    @pl.when(i < n_prompt)
    def _():
        outp_ref[...] = y

    @pl.when(i >= n_prompt)
    def _():
        outs_ref[...] = y


def _combine_ln(h32, y_slots, gates, shared, ln_g, ln_b, alpha, m_prompt):
    m, d = h32.shape
    m_sample = m - m_prompt
    tm = _pick_tile(math.gcd(m_prompt, m_sample), 128, SUBLANES)
    n_prompt = m_prompt // tm
    nt = m // tm
    row = pl.BlockSpec((tm, d), lambda i: (i, 0))
    vec = pl.BlockSpec((1, d), lambda i: (0, 0))
    slot_rows = [pl.BlockSpec((tm, d // 2), functools.partial(lambda k, i: (k * nt + i, 0), k))
                 for k in range(TOP_K)]
    return pl.pallas_call(
        functools.partial(_combine_ln_kernel, alpha=alpha, n_prompt=n_prompt),
        grid=(nt,),
        in_specs=[row, *slot_rows, pl.BlockSpec((tm, TOP_K), lambda i: (i, 0)), row, vec, vec],
        out_specs=[pl.BlockSpec((tm, d), lambda i: (jnp.minimum(i, n_prompt - 1), 0)),
                   pl.BlockSpec((tm, d), lambda i: (jnp.maximum(i - n_prompt, 0), 0))],
        out_shape=[jax.ShapeDtypeStruct((m_prompt, d), F32), jax.ShapeDtypeStruct((m_sample, d), F32)],
        compiler_params=_params("arbitrary"),
        name="combine_ln",
    )(h32, *([y_slots] * TOP_K), gates, shared, ln_g, ln_b)


def _layer(xp, xs, st_pool, st_conv, alpha, layer, w_in, w_pool, pool_scale, conv_w, w_o,
           ln1_g, ln1_b, ln2_g, ln2_b, w_router, router_bias,
           w_gate, w_up, w_down, ws_gate, ws_up, ws_down):
    batch, seq, d = xp.shape
    bs = xs.shape[0]
    pool_buf, pw = st_pool.shape[1], st_pool.shape[2]
    conv_buf, cw = st_conv.shape[1], st_conv.shape[2]
    n_experts = w_router.shape[1]
    pg = w_pool.shape[1]

    xp2 = xp.reshape(batch * seq, d)
    xs2 = xs.reshape(bs, d)
    w_in16 = w_in.astype(BF16)
    w_pool16 = w_pool.astype(BF16)
    scale2 = pool_scale.reshape(1, pw)

    xp16 = xp2.astype(BF16)
    pool_p, new_pool_p = _pool_prompt(xp16, w_in16, w_pool16, scale2, batch, seq, pool_buf)
    conv_p, new_conv_p = _conv_prompt(xp16, w_in16, conv_w, batch, seq, pw, cw, pg, conv_buf)

    stp_t = jnp.transpose(st_pool, (1, 0, 2))
    stc_t = jnp.transpose(st_conv, (1, 0, 2))
    pool_s, conv_s, u_s, v_s = _mixer_sample(xs2.astype(BF16), w_in16, stp_t, stc_t, conv_w,
                                             w_pool16, scale2)
    new_pool_s = jnp.concatenate([st_pool[:, 1:], u_s[:, None, :]], axis=1)
    new_conv_s = jnp.concatenate([st_conv[:, 1:], v_s[:, None, :]], axis=1)

    x_all = jnp.concatenate([xp2, xs2], axis=0)
    pool_all = jnp.concatenate([pool_p, pool_s], axis=0)
    conv_all = jnp.concatenate([conv_p, conv_s], axis=0)
    h32, h16, hp, scores = _outproj(pool_all, conv_all, w_o.astype(BF16), x_all,
                                    ln1_g.reshape(1, d), ln1_b.reshape(1, d),
                                    w_router.astype(BF16), alpha)

    gates, order, n_items, item_e, item_ep, item_start, item_cnt = _route(scores, router_bias, n_experts)
    y_slots = _moe_routed(hp, order, n_items, item_e, item_ep, item_start, item_cnt,
                          w_gate, w_up, w_down, layer)
    shared = _shared_ffn(h16, ws_gate.astype(BF16), ws_up.astype(BF16), ws_down.astype(BF16))
    yp, ys = _combine_ln(h32, y_slots, gates, shared,
                         ln2_g.reshape(1, d), ln2_b.reshape(1, d), alpha, batch * seq)
    return (yp.reshape(batch, seq, d), ys.reshape(bs, xs.shape[1], d),
            new_pool_p, new_pool_s, new_conv_p, new_conv_s)


def kernel(x_prompt, x_sample, state_pool, state_conv, w_in, w_pool, pool_scale, conv_w, w_o,
           ln1_g, ln1_b, ln2_g, ln2_b, w_router, router_bias,
           w_gate, w_up, w_down, ws_gate, ws_up, ws_down):
    depth = w_in.shape[0]
    assert x_sample.shape[1] == 1, "decode rows carry one new token per sequence"
    alpha = (2.0 * depth) ** 0.25
    hp, hs = x_prompt, x_sample
    pool_p, pool_s, conv_p, conv_s = [], [], [], []
    for l in range(depth):
        hp, hs, npp, nps, ncp, ncs = _layer(
            hp, hs, state_pool[l], state_conv[l], alpha, l, w_in[l], w_pool[l], pool_scale[l],
            conv_w[l], w_o[l], ln1_g[l], ln1_b[l], ln2_g[l], ln2_b[l], w_router[l],
            router_bias[l], w_gate, w_up, w_down, ws_gate[l], ws_up[l], ws_down[l])
        pool_p.append(npp)
        pool_s.append(nps)
        conv_p.append(ncp)
        conv_s.append(ncs)
    return (hp, hs, jnp.stack(pool_p, 0), jnp.stack(pool_s, 0), jnp.stack(conv_p, 0), jnp.stack(conv_s, 0))
```

```python
import functools
import math

import jax
import jax.numpy as jnp
from jax import lax
from jax.experimental import pallas as pl
from jax.experimental.pallas import tpu as pltpu

POOL_WINDOWS = (2, 4, 8, 16)
TOP_K = 8
ROUTED_SCALE = 2.5
LN_EPS = 1e-5
PAST_LEN = 16384

LANES = 128
SUBLANES = 8
VMEM_LIMIT = 56 * 1024 * 1024

POOL_HALO = 32
CONV_HALO = 8
ITEM_ROWS = 512
ROW_GRAIN = 64
TOP_K_SHIFT = 3
assert 1 << TOP_K_SHIFT == TOP_K and ITEM_ROWS % ROW_GRAIN == 0

BF16 = jnp.bfloat16
F32 = jnp.float32


def _params(*sem):
    return pltpu.CompilerParams(dimension_semantics=sem, vmem_limit_bytes=VMEM_LIMIT)


def _pick_tile(n, target, mult):
    best = None
    for t in range(mult, min(n, target) + 1, mult):
        if n % t == 0:
            best = t
    assert best is not None, (n, target, mult)
    return best


def _dot(a, b):
    return jnp.dot(a, b, preferred_element_type=F32)


def _silu(x):
    return x * jax.nn.sigmoid(x)


def _pack_halves(x16):
    half = x16.shape[1] // 2
    return _pack_pair(x16[:, :half], x16[:, half:])


def _pack_pair(lo16, hi16):
    lo = lax.bitcast_convert_type(lo16.astype(F32), jnp.uint32)
    hi = lax.bitcast_convert_type(hi16.astype(F32), jnp.uint32)
    return (lo >> 16) | hi


def _unpack_half(words, high):
    bits = jnp.where(high, words & jnp.uint32(0xFFFF0000), words << 16)
    return lax.bitcast_convert_type(bits, F32).astype(BF16)


def _pool_prompt_kernel(x_ref, w_ref, wp_ref, sc_ref, out_ref, st_ref, u_buf, a_buf, b_buf,
                        *, tm, windows, pool_buf):
    g = pl.program_id(0)
    t = pl.program_id(2)
    nt = pl.num_programs(2)
    halo = POOL_HALO
    rows = halo + tm

    @pl.when(t == 0)
    def _():
        u_buf[0:halo, :] = jnp.zeros((halo, u_buf.shape[1]), F32)

    u = _dot(x_ref[...], w_ref[...])
    u_buf[halo:rows, :] = u
    pos1 = (t * tm + lax.broadcasted_iota(jnp.int32, (tm, 1), 0) + 1).astype(F32)

    for gi, w in enumerate(windows):
        @pl.when(g == gi)
        def _(w=w):
            src, shift, start = u_buf, 1, SUBLANES
            dsts = (a_buf, b_buf)
            lev = 0
            while shift < w:
                dst = dsts[lev % 2]
                n = rows - start
                dst[start:rows, :] = src[start:rows, :] + src[start - shift:start - shift + n, :]
                src, shift, start, lev = dst, shift * 2, start + SUBLANES, lev + 1
            win = src[halo:rows, :]
            inv_cnt = 1.0 / jnp.minimum(pos1, float(w))
            pooled = (win * inv_cnt - u).astype(BF16)
            out_ref[...] = (_dot(pooled, wp_ref[...]) * sc_ref[...]).astype(out_ref.dtype)

    @pl.when(t == nt - 1)
    def _():
        st_ref[...] = u_buf[rows - pool_buf:rows, :]

    u_buf[0:halo, :] = u_buf[tm:rows, :]


def _pool_prompt(x16, w_in16, w_pool16, pool_scale, batch, seq, pool_buf):
    m, d = x16.shape
    ngrp, pg, _ = w_pool16.shape
    pw = ngrp * pg
    assert max(POOL_WINDOWS) <= POOL_HALO and len(POOL_WINDOWS) == ngrp and pg % LANES == 0
    tm = _pick_tile(seq, 1024, 16)
    nt = seq // tm
    kern = functools.partial(_pool_prompt_kernel, tm=tm, windows=POOL_WINDOWS, pool_buf=pool_buf)
    return pl.pallas_call(
        kern,
        grid=(ngrp, batch, nt),
        in_specs=[
            pl.BlockSpec((tm, d), lambda g, b, t: (b * nt + t, 0)),
            pl.BlockSpec((d, pg), lambda g, b, t: (0, g)),
            pl.BlockSpec((None, pg, pg), lambda g, b, t: (g, 0, 0)),
            pl.BlockSpec((1, pg), lambda g, b, t: (0, g)),
        ],
        out_specs=[
            pl.BlockSpec((tm, pg), lambda g, b, t: (b * nt + t, g)),
            pl.BlockSpec((None, pool_buf, pg), lambda g, b, t: (b, 0, g)),
        ],
        out_shape=[
            jax.ShapeDtypeStruct((m, pw), BF16),
            jax.ShapeDtypeStruct((batch, pool_buf, pw), F32),
        ],
        scratch_shapes=[pltpu.VMEM((POOL_HALO + tm, pg), F32)] * 3,
        compiler_params=_params("arbitrary", "arbitrary", "arbitrary"),
        name="pool_prompt",
    )(x16, w_in16, w_pool16, pool_scale)


def _conv_prompt_kernel(x_ref, wb_ref, wc_ref, wh_ref, cw_ref, out_ref, st_ref, v_buf,
                        *, tm, conv_buf):
    t = pl.program_id(2)
    nt = pl.num_programs(2)
    halo = CONV_HALO
    rows = halo + tm

    @pl.when(t == 0)
    def _():
        v_buf[0:halo, :] = jnp.zeros((halo, v_buf.shape[1]), F32)

    x = x_ref[...]
    v = _dot(x, wc_ref[...]) * _dot(x, wh_ref[...])
    v_buf[halo:rows, :] = v
    conv = (cw_ref[0:1, :] * v_buf[halo - 2:rows - 2, :]
            + cw_ref[1:2, :] * v_buf[halo - 1:rows - 1, :]
            + cw_ref[2:3, :] * v)
    out_ref[...] = (_dot(x, wb_ref[...]) * conv).astype(out_ref.dtype)

    @pl.when(t == nt - 1)
    def _():
        st_ref[...] = v_buf[rows - conv_buf:rows, :]

    v_buf[0:halo, :] = v_buf[tm:rows, :]


def _conv_prompt(x16, w_in16, conv_w, batch, seq, pw, cw, ct, conv_buf):
    m, d = x16.shape
    assert conv_w.shape[0] == 3 and conv_buf == 2 and cw % ct == 0 and pw % ct == 0
    nc = cw // ct
    off_b, off_c, off_h = pw // ct, (pw + cw) // ct, (pw + 2 * cw) // ct
    tm = _pick_tile(seq, 1024, 16)
    nt = seq // tm
    kern = functools.partial(_conv_prompt_kernel, tm=tm, conv_buf=conv_buf)
    return pl.pallas_call(
        kern,
        grid=(nc, batch, nt),
        in_specs=[
            pl.BlockSpec((tm, d), lambda c, b, t: (b * nt + t, 0)),
            pl.BlockSpec((d, ct), lambda c, b, t: (0, off_b + c)),
            pl.BlockSpec((d, ct), lambda c, b, t: (0, off_c + c)),
            pl.BlockSpec((d, ct), lambda c, b, t: (0, off_h + c)),
            pl.BlockSpec((3, ct), lambda c, b, t: (0, c)),
        ],
        out_specs=[
            pl.BlockSpec((tm, ct), lambda c, b, t: (b * nt + t, c)),
            pl.BlockSpec((None, conv_buf, ct), lambda c, b, t: (b, 0, c)),
        ],
        out_shape=[
            jax.ShapeDtypeStruct((m, cw), BF16),
            jax.ShapeDtypeStruct((batch, conv_buf, cw), F32),
        ],
        scratch_shapes=[pltpu.VMEM((CONV_HALO + tm, ct), F32)],
        compiler_params=_params("arbitrary", "arbitrary", "arbitrary"),
        name="conv_prompt",
    )(x16, w_in16, w_in16, w_in16, conv_w)


def _mixer_sample_kernel(x_ref, wu_ref, wb_ref, wc_ref, wh_ref, stp_ref, stc_ref, cw_ref, wp_ref,
                         sc_ref, pool_ref, conv_ref, u_ref, v_ref, *, windows, pool_buf):
    c = pl.program_id(0)
    x = x_ref[...]
    u = _dot(x, wu_ref[...])
    u_ref[...] = u
    for gi, w in enumerate(windows):
        @pl.when(c == gi)
        def _(w=w):
            win = u
            for j in range(1, w):
                win = win + stp_ref[pool_buf - j]
            cnt = float(min(PAST_LEN + 1, w))
            pooled = (win / cnt - u).astype(BF16)
            pool_ref[...] = (_dot(pooled, wp_ref[...]) * sc_ref[...]).astype(pool_ref.dtype)

    v = _dot(x, wc_ref[...]) * _dot(x, wh_ref[...])
    v_ref[...] = v
    conv = cw_ref[0:1, :] * stc_ref[0] + cw_ref[1:2, :] * stc_ref[1] + cw_ref[2:3, :] * v
    conv_ref[...] = (_dot(x, wb_ref[...]) * conv).astype(conv_ref.dtype)


def _mixer_sample(x16, w_in16, stp_t, stc_t, conv_w, w_pool16, pool_scale):
    bs, d = x16.shape
    ngrp, pg, _ = w_pool16.shape
    pool_buf, conv_buf = stp_t.shape[0], stc_t.shape[0]
    pw, cw = stp_t.shape[2], stc_t.shape[2]
    assert cw == pw, "decode mixer walks pooling and conv column tiles together"
    kern = functools.partial(_mixer_sample_kernel, windows=POOL_WINDOWS, pool_buf=pool_buf)
    col = lambda off: pl.BlockSpec((d, pg), lambda c: (0, off + c))
    tile = pl.BlockSpec((bs, pg), lambda c: (0, c))
    return pl.pallas_call(
        kern,
        grid=(ngrp,),
        in_specs=[
            pl.BlockSpec((bs, d), lambda c: (0, 0)),
            col(0), col(pw // pg), col((pw + cw) // pg), col((pw + 2 * cw) // pg),
            pl.BlockSpec((pool_buf, bs, pg), lambda c: (0, 0, c)),
            pl.BlockSpec((conv_buf, bs, pg), lambda c: (0, 0, c)),
            pl.BlockSpec((3, pg), lambda c: (0, c)),
            pl.BlockSpec((None, pg, pg), lambda c: (c, 0, 0)),
            pl.BlockSpec((1, pg), lambda c: (0, c)),
        ],
        out_specs=[tile, tile, tile, tile],
        out_shape=[
            jax.ShapeDtypeStruct((bs, pw), BF16),
            jax.ShapeDtypeStruct((bs, cw), BF16),
            jax.ShapeDtypeStruct((bs, pw), F32),
            jax.ShapeDtypeStruct((bs, cw), F32),
        ],
        compiler_params=_params("arbitrary"),
        name="mixer_sample",
    )(x16, w_in16, w_in16, w_in16, w_in16, stp_t, stc_t, conv_w, w_pool16, pool_scale)


def _layer_norm(r, g, b):
    mu = jnp.mean(r, axis=-1, keepdims=True)
    d = r - mu
    var = jnp.mean(d * d, axis=-1, keepdims=True)
    return d * lax.rsqrt(var + LN_EPS) * g + b


def _outproj_kernel(pool_ref, conv_ref, wo_ref, x_ref, g_ref, b_ref, wr_ref, *rest, alpha, nk_pool):
    h32_ref, h16_ref, hp_ref, sc_ref, acc = rest[-5:]
    k = pl.program_id(1)
    nk = pl.num_programs(1)

    @pl.when(k == 0)
    def _():
        acc[...] = _dot(pool_ref[...], wo_ref[...])

    @pl.when(jnp.logical_and(k > 0, k < nk_pool))
    def _():
        acc[...] += _dot(pool_ref[...], wo_ref[...])

    @pl.when(k >= nk_pool)
    def _():
        acc[...] += _dot(conv_ref[...], wo_ref[...])

    xc = x_ref.shape[1]
    cols = pl.ds(pl.multiple_of(k * xc, xc), xc)
    acc[:, cols] += alpha * x_ref[...]

    @pl.when(k == nk - 1)
    def _():
        h = _layer_norm(acc[...], g_ref[...], b_ref[...])
        h32_ref[...] = h
        h16 = h.astype(BF16)
        h16_ref[...] = h16
        hp_ref[...] = _pack_halves(h16)
        sc_ref[...] = jax.nn.sigmoid(_dot(h16, wr_ref[...]))


def _outproj(pool, conv, w_o16, x, ln_g, ln_b, w_r16, alpha, m_total, row0, fill=None):
    m, pw = pool.shape
    cw = conv.shape[1]
    d = w_o16.shape[1]
    e = w_r16.shape[1]
    tm = _pick_tile(math.gcd(m, row0) if row0 else m, 256, 16)
    tk = _pick_tile(math.gcd(pw, cw), 512, LANES)
    nkp, nkc = pw // tk, cw // tk
    xc = d // (nkp + nkc)
    assert xc % LANES == 0 and row0 % tm == 0 and row0 + m <= m_total
    blk0 = row0 // tm
    kern = functools.partial(_outproj_kernel, alpha=alpha, nk_pool=nkp)
    row = lambda width: pl.BlockSpec((tm, width), lambda i, k: (blk0 + i, 0))
    vec = pl.BlockSpec((1, d), lambda i, k: (0, 0))
    fill = () if fill is None else tuple(fill)
    n_in = 7
    return pl.pallas_call(
        kern,
        grid=(m // tm, nkp + nkc),
        in_specs=[
            pl.BlockSpec((tm, tk), lambda i, k: (i, jnp.minimum(k, nkp - 1))),
            pl.BlockSpec((tm, tk), lambda i, k: (i, jnp.maximum(k - nkp, 0))),
            pl.BlockSpec((tk, d), lambda i, k: (k, 0)),
            pl.BlockSpec((tm, xc), lambda i, k: (i, k)), vec, vec,
            pl.BlockSpec((d, e), lambda i, k: (0, 0)),
        ] + [pl.BlockSpec(memory_space=pl.ANY)] * len(fill),
        out_specs=[row(d), row(d), row(d // 2), row(e)],
        out_shape=[
            jax.ShapeDtypeStruct((m_total, d), F32),
            jax.ShapeDtypeStruct((m_total, d), BF16),
            jax.ShapeDtypeStruct((m_total, d // 2), jnp.uint32),
            jax.ShapeDtypeStruct((m_total, e), F32),
        ],
        input_output_aliases={n_in + j: j for j in range(len(fill))},
        scratch_shapes=[pltpu.VMEM((tm, d), F32)],
        compiler_params=_params("arbitrary", "arbitrary"),
        name="outproj_ln_router",
    )(pool, conv, w_o16, x, ln_g, ln_b, w_r16, *fill)


def _slot_row(a, m_tok):
    return (a & (TOP_K - 1)) * m_tok + _token_of(a)


def _token_of(a):
    return lax.shift_right_logical(a, TOP_K_SHIFT)


def _moe_kernel(order_ref, e_ref, ep_ref, st_ref, cnt_ref,
                hp_hbm, wg_ref, wu_ref, wd_ref,
                y_hbm,
                xbuf, gu_acc, hb16, ylow, ypack, gsem, ssem,
                *, ka, nb, tk, tn, de, m_tok):
    del e_ref, ep_ref
    i = pl.program_id(0)
    s = pl.program_id(1)
    ni = pl.num_programs(0)
    cnt = cnt_ref[i]
    slot = i % 2
    variant = (cnt + ROW_GRAIN - 1) // ROW_GRAIN

    def gather_copy(item_slot, j, tok):
        return pltpu.make_async_copy(hp_hbm.at[pl.ds(tok, 1), :], xbuf.at[item_slot, pl.ds(j, 1), :],
                                     gsem.at[item_slot])

    def scatter_copy(j, row):
        return pltpu.make_async_copy(ypack.at[pl.ds(j, 1), :], y_hbm.at[pl.ds(row, 1), :], ssem.at[0])

    def share(count, part, parts):
        chunk = (count + parts - 1) // parts
        lo = jnp.minimum(part * chunk, count)
        return lo, jnp.minimum(lo + chunk, count)

    def start_gather(item, item_slot, lo, hi):
        base = st_ref[item]

        def body(j, carry):
            gather_copy(item_slot, j, _token_of(order_ref[base + j])).start()
            return carry
        lax.fori_loop(lo, hi, body, 0)

    def start_scatter(item, lo, hi):
        base = st_ref[item]

        def body(j, carry):
            scatter_copy(j, _slot_row(order_ref[base + j], m_tok)).start()
            return carry
        lax.fori_loop(lo, hi, body, 0)

    def wait_rows(count, block_copy):
        p = ITEM_ROWS
        while p >= 1:
            pl.when((count & p) != 0)(functools.partial(lambda q: block_copy(q).wait(), p))
            p //= 2

    def wait_gather(item, item_slot):
        wait_rows(cnt_ref[item], lambda p: pltpu.make_async_copy(
            hp_hbm.at[pl.ds(0, p), :], xbuf.at[item_slot, pl.ds(0, p), :], gsem.at[item_slot]))

    def wait_scatter(item):
        wait_rows(cnt_ref[item], lambda p: pltpu.make_async_copy(
            ypack.at[pl.ds(0, p), :], y_hbm.at[pl.ds(0, p), :], ssem.at[0]))

    def for_variant(fn):
        for v in range(1, ITEM_ROWS // ROW_GRAIN + 1):
            pl.when(variant == v)(functools.partial(fn, v * ROW_GRAIN))

    @pl.when(s == 0)
    def _():
        @pl.when(i == 0)
        def _():
            xbuf[...] = jnp.zeros(xbuf.shape, xbuf.dtype)
            start_gather(0, 0, 0, cnt)

        wait_gather(i, slot)

    @pl.when(i + 1 < ni)
    def _():
        lo, hi = share(cnt_ref[i + 1], s, ka + nb)
        start_gather(i + 1, 1 - slot, lo, hi)

    @pl.when(jnp.logical_and(s < ka, i > 0))
    def _():
        prev = jnp.maximum(i - 1, 0)
        lo, hi = share(cnt_ref[prev], s, ka)
        start_scatter(prev, lo, hi)

    @pl.when(s < ka)
    def _():
        chunks_per_half = ka // 2
        high = s >= chunks_per_half
        coff = pl.multiple_of((s % chunks_per_half) * tk, tk)

        def gate_up(rows):
            x16 = _unpack_half(xbuf[slot, 0:rows, pl.ds(coff, tk)], high)
            g = _dot(x16, wg_ref[...].astype(BF16))
            u = _dot(x16, wu_ref[...].astype(BF16))

            @pl.when(s == 0)
            def _():
                gu_acc[0:rows, 0:de] = g
                gu_acc[0:rows, de:2 * de] = u

            @pl.when(s > 0)
            def _():
                gu_acc[0:rows, 0:de] += g
                gu_acc[0:rows, de:2 * de] += u
        for_variant(gate_up)

    @pl.when(s >= ka)
    def _():
        n = s - ka
        half_blks = nb // 2
        woff = pl.multiple_of((n % half_blks) * tn, tn)

        @pl.when(n == 0)
        def _():
            @pl.when(i > 0)
            def _():
                wait_scatter(jnp.maximum(i - 1, 0))

            def act(rows):
                hb16[0:rows, :] = (_silu(gu_acc[0:rows, 0:de]) * gu_acc[0:rows, de:2 * de]).astype(BF16)
            for_variant(act)

        def down(rows):
            y = _dot(hb16[0:rows, :], wd_ref[...].astype(BF16))

            @pl.when(n < half_blks)
            def _():
                ylow[0:rows, pl.ds(woff, tn)] = y

            @pl.when(n >= half_blks)
            def _():
                ypack[0:rows, pl.ds(woff, tn)] = _pack_pair(ylow[0:rows, pl.ds(woff, tn)].astype(BF16),
                                                             y.astype(BF16))
        for_variant(down)

        @pl.when(jnp.logical_and(n == nb - 1, i == ni - 1))
        def _():
            start_scatter(i, 0, cnt)
            wait_scatter(i)


def _moe_routed(hp, order, n_items, item_e, item_eprev, item_start, item_cnt, w_gate, w_up, w_down, layer):
    m_tok, dh = hp.shape
    d = 2 * dh
    de = w_gate.shape[-1]
    tk = _pick_tile(dh, min(1024, d // 4), LANES)
    tn = tk
    ka, nb = d // tk, d // tn
    assert de % LANES == 0 and nb >= 2 and ka % 2 == 0 and nb % 2 == 0

    def wgu_map(i, s, order, e, ep, st, cnt):
        return (layer, e[i], jnp.minimum(s, ka - 1), 0)

    def wd_map(i, s, order, e, ep, st, cnt):
        first = s < ka
        return (layer, jnp.where(first, ep[i], e[i]), 0, jnp.where(first, nb - 1, s - ka))

    kern = functools.partial(_moe_kernel, ka=ka, nb=nb, tk=tk, tn=tn, de=de, m_tok=m_tok)
    grid_spec = pltpu.PrefetchScalarGridSpec(
        num_scalar_prefetch=5,
        grid=(n_items[0], ka + nb),
        in_specs=[
            pl.BlockSpec(memory_space=pl.ANY),
            pl.BlockSpec((None, None, tk, de), wgu_map),
            pl.BlockSpec((None, None, tk, de), wgu_map),
            pl.BlockSpec((None, None, de, tn), wd_map),
        ],
        out_specs=pl.BlockSpec(memory_space=pl.ANY),
        scratch_shapes=[
            pltpu.VMEM((2, ITEM_ROWS, dh), jnp.uint32),
            pltpu.VMEM((ITEM_ROWS, 2 * de), F32),
            pltpu.VMEM((ITEM_ROWS, de), BF16),
            pltpu.VMEM((ITEM_ROWS, dh), F32),
            pltpu.VMEM((ITEM_ROWS, dh), jnp.uint32),
            pltpu.SemaphoreType.DMA((2,)),
            pltpu.SemaphoreType.DMA((1,)),
        ],
    )
    return pl.pallas_call(
        kern,
        grid_spec=grid_spec,
        out_shape=jax.ShapeDtypeStruct((TOP_K * m_tok, dh), jnp.uint32),
        compiler_params=_params("arbitrary", "arbitrary"),
        name="moe_routed",
    )(order, item_e, item_eprev, item_start, item_cnt, hp, w_gate, w_up, w_down)


def _route(scores, bias, n_experts):
    m = scores.shape[0]
    a = m * TOP_K
    _, idx = lax.top_k(scores + bias[None, :].astype(F32), TOP_K)
    sel = jnp.take_along_axis(scores, idx, axis=1)
    gates = sel / jnp.sum(sel, axis=-1, keepdims=True) * ROUTED_SCALE

    abits = max(1, (a - 1).bit_length())
    assert (n_experts << abits) < 2 ** 31
    e_flat = idx.reshape(-1).astype(jnp.int32)
    keys = lax.sort((e_flat << abits) | jnp.arange(a, dtype=jnp.int32))
    order = keys & ((1 << abits) - 1)
    experts = jnp.arange(n_experts, dtype=jnp.int32)
    counts = jnp.sum((idx.astype(jnp.int32)[:, :, None] == experts[None, None, :]).astype(jnp.int32), axis=(0, 1))
    starts = jnp.cumsum(counts) - counts

    ni = n_experts + a // ITEM_ROWS
    n_items = (counts + ITEM_ROWS - 1) // ITEM_ROWS
    item_end = jnp.cumsum(n_items)
    total = item_end[-1]
    slots = jnp.arange(ni, dtype=jnp.int32)
    live = slots < total
    sl = jnp.minimum(slots, total - 1)
    ex = jnp.sum((item_end[None, :] <= sl[:, None]).astype(jnp.int32), axis=1)
    jj = sl - (item_end[ex] - n_items[ex])
    item_start = starts[ex] + jj * ITEM_ROWS
    item_cnt = jnp.where(live, jnp.minimum(ITEM_ROWS, counts[ex] - jj * ITEM_ROWS), 0)
    ex_prev = jnp.concatenate([ex[:1], ex[:-1]])
    return (gates, order.astype(jnp.int32), total.reshape(1).astype(jnp.int32), ex.astype(jnp.int32),
            ex_prev.astype(jnp.int32), item_start.astype(jnp.int32), item_cnt.astype(jnp.int32))


def _shared_kernel(h_ref, wgu_ref, wd_ref, out_ref, *, ch):
    c = pl.program_id(1)
    gu = _dot(h_ref[...], wgu_ref[...])
    hb = (_silu(gu[:, 0:ch]) * gu[:, ch:2 * ch]).astype(BF16)
    part = _dot(hb, wd_ref[...])

    @pl.when(c == 0)
    def _():
        out_ref[...] = part

    @pl.when(c > 0)
    def _():
        out_ref[...] += part


def _shared_ffn(h16, ws_gate16, ws_up16, ws_down16):
    m, d = h16.shape
    de = ws_gate16.shape[1]
    ch = _pick_tile(de, 384, LANES)
    nc = de // ch
    tm = _pick_tile(m, 416, 16)
    wgu = jnp.concatenate(
        [jnp.concatenate([ws_gate16[:, c * ch:(c + 1) * ch], ws_up16[:, c * ch:(c + 1) * ch]], axis=1)
         for c in range(nc)], axis=1)
    return pl.pallas_call(
        functools.partial(_shared_kernel, ch=ch),
        grid=(m // tm, nc),
        in_specs=[
            pl.BlockSpec((tm, d), lambda i, c: (i, 0)),
            pl.BlockSpec((d, 2 * ch), lambda i, c: (0, c)),
            pl.BlockSpec((ch, d), lambda i, c: (c, 0)),
        ],
        out_specs=pl.BlockSpec((tm, d), lambda i, c: (i, 0)),
        out_shape=jax.ShapeDtypeStruct((m, d), F32),
        compiler_params=_params("arbitrary", "arbitrary"),
        name="shared_ffn",
    )(h16, wgu, ws_down16)


def _combine_ln_kernel(h_ref, *refs, alpha, n_prompt):
    y_refs = refs[:TOP_K]
    gt_ref, s_ref, g_ref, b_ref, outp_ref, outs_ref = refs[TOP_K:]
    i = pl.program_id(0)
    lo = hi = None
    for k in range(TOP_K):
        words = y_refs[k][...]
        gate = gt_ref[:, k:k + 1]
        lo_k = gate * lax.bitcast_convert_type(words << 16, F32)
        hi_k = gate * lax.bitcast_convert_type(words & jnp.uint32(0xFFFF0000), F32)
        lo = lo_k if lo is None else lo + lo_k
        hi = hi_k if hi is None else hi + hi_k
    routed = jnp.concatenate([lo, hi], axis=1)
    y = _layer_norm(alpha * h_ref[...] + (routed + s_ref[...]), g_ref[...], b_ref[...])

    @pl.when(i < n_prompt)
    def _():
        outp_ref[...] = y

    @pl.when(i >= n_prompt)
    def _():
        outs_ref[...] = y


def _combine_ln(h32, y_slots, gates, shared, ln_g, ln_b, alpha, m_prompt):
    m, d = h32.shape
    m_sample = m - m_prompt
    tm = _pick_tile(math.gcd(m_prompt, m_sample), 128, SUBLANES)
    n_prompt = m_prompt // tm
    nt = m // tm
    row = pl.BlockSpec((tm, d), lambda i: (i, 0))
    vec = pl.BlockSpec((1, d), lambda i: (0, 0))
    slot_rows = [pl.BlockSpec((tm, d // 2), functools.partial(lambda k, i: (k * nt + i, 0), k))
                 for k in range(TOP_K)]
    return pl.pallas_call(
        functools.partial(_combine_ln_kernel, alpha=alpha, n_prompt=n_prompt),
        grid=(nt,),
        in_specs=[row, *slot_rows, pl.BlockSpec((tm, TOP_K), lambda i: (i, 0)), row, vec, vec],
        out_specs=[pl.BlockSpec((tm, d), lambda i: (jnp.minimum(i, n_prompt - 1), 0)),
                   pl.BlockSpec((tm, d), lambda i: (jnp.maximum(i - n_prompt, 0), 0))],
        out_shape=[jax.ShapeDtypeStruct((m_prompt, d), F32), jax.ShapeDtypeStruct((m_sample, d), F32)],
        compiler_params=_params("arbitrary"),
        name="combine_ln",
    )(h32, *([y_slots] * TOP_K), gates, shared, ln_g, ln_b)


def _layer(xp, xs, st_pool, st_conv, alpha, layer, w_in, w_pool, pool_scale, conv_w, w_o,
           ln1_g, ln1_b, ln2_g, ln2_b, w_router, router_bias,
           w_gate, w_up, w_down, ws_gate, ws_up, ws_down):
    batch, seq, d = xp.shape
    bs = xs.shape[0]
    pool_buf, pw = st_pool.shape[1], st_pool.shape[2]
    conv_buf, cw = st_conv.shape[1], st_conv.shape[2]
    n_experts = w_router.shape[1]
    pg = w_pool.shape[1]

    xp2 = xp.reshape(batch * seq, d)
    xs2 = xs.reshape(bs, d)
    w_in16 = w_in.astype(BF16)
    w_pool16 = w_pool.astype(BF16)
    scale2 = pool_scale.reshape(1, pw)

    xp16 = xp2.astype(BF16)
    pool_p, new_pool_p = _pool_prompt(xp16, w_in16, w_pool16, scale2, batch, seq, pool_buf)
    conv_p, new_conv_p = _conv_prompt(xp16, w_in16, conv_w, batch, seq, pw, cw, pg, conv_buf)

    stp_t = jnp.transpose(st_pool, (1, 0, 2))
    stc_t = jnp.transpose(st_conv, (1, 0, 2))
    pool_s, conv_s, u_s, v_s = _mixer_sample(xs2.astype(BF16), w_in16, stp_t, stc_t, conv_w,
                                             w_pool16, scale2)
    new_pool_s = jnp.concatenate([st_pool[:, 1:], u_s[:, None, :]], axis=1)
    new_conv_s = jnp.concatenate([st_conv[:, 1:], v_s[:, None, :]], axis=1)

    m_prompt, m_all = batch * seq, batch * seq + bs
    w_o16, w_r16 = w_o.astype(BF16), w_router.astype(BF16)
    g1, b1 = ln1_g.reshape(1, d), ln1_b.reshape(1, d)
    outs = _outproj(pool_p, conv_p, w_o16, xp2, g1, b1, w_r16, alpha, m_all, 0)
    h32, h16, hp, scores = _outproj(pool_s, conv_s, w_o16, xs2, g1, b1, w_r16, alpha, m_all, m_prompt,
                                    fill=outs)

    gates, order, n_items, item_e, item_ep, item_start, item_cnt = _route(scores, router_bias, n_experts)
    y_slots = _moe_routed(hp, order, n_items, item_e, item_ep, item_start, item_cnt,
                          w_gate, w_up, w_down, layer)
    shared = _shared_ffn(h16, ws_gate.astype(BF16), ws_up.astype(BF16), ws_down.astype(BF16))
    yp, ys = _combine_ln(h32, y_slots, gates, shared,
                         ln2_g.reshape(1, d), ln2_b.reshape(1, d), alpha, batch * seq)
    return (yp.reshape(batch, seq, d), ys.reshape(bs, xs.shape[1], d),
            new_pool_p, new_pool_s, new_conv_p, new_conv_s)


def kernel(x_prompt, x_sample, state_pool, state_conv, w_in, w_pool, pool_scale, conv_w, w_o,
           ln1_g, ln1_b, ln2_g, ln2_b, w_router, router_bias,
           w_gate, w_up, w_down, ws_gate, ws_up, ws_down):
    depth = w_in.shape[0]
    assert x_sample.shape[1] == 1, "decode rows carry one new token per sequence"
    alpha = (2.0 * depth) ** 0.25
    hp, hs = x_prompt, x_sample
    pool_p, pool_s, conv_p, conv_s = [], [], [], []
    for l in range(depth):
        hp, hs, npp, nps, ncp, ncs = _layer(
            hp, hs, state_pool[l], state_conv[l], alpha, l, w_in[l], w_pool[l], pool_scale[l],
            conv_w[l], w_o[l], ln1_g[l], ln1_b[l], ln2_g[l], ln2_b[l], w_router[l],
            router_bias[l], w_gate, w_up, w_down, ws_gate[l], ws_up[l], ws_down[l])
        pool_p.append(npp)
        pool_s.append(nps)
        conv_p.append(ncp)
        conv_s.append(ncs)
    return (hp, hs, jnp.stack(pool_p, 0), jnp.stack(pool_s, 0), jnp.stack(conv_p, 0), jnp.stack(conv_s, 0))
```

```python
import functools
import math

import jax
import jax.numpy as jnp
from jax import lax
from jax.experimental import pallas as pl
from jax.experimental.pallas import tpu as pltpu

POOL_WINDOWS = (2, 4, 8, 16)
TOP_K = 8
ROUTED_SCALE = 2.5
LN_EPS = 1e-5
PAST_LEN = 16384

LANES = 128
SUBLANES = 8
VMEM_LIMIT = 56 * 1024 * 1024

POOL_HALO = 32
CONV_HALO = 8
ITEM_ROWS = 512
ROW_GRAIN = 64
TOP_K_SHIFT = 3
assert 1 << TOP_K_SHIFT == TOP_K and ITEM_ROWS % ROW_GRAIN == 0

BF16 = jnp.bfloat16
F32 = jnp.float32


def _params(*sem):
    return pltpu.CompilerParams(dimension_semantics=sem, vmem_limit_bytes=VMEM_LIMIT)


def _pick_tile(n, target, mult):
    best = None
    for t in range(mult, min(n, target) + 1, mult):
        if n % t == 0:
            best = t
    assert best is not None, (n, target, mult)
    return best


def _dot(a, b):
    return jnp.dot(a, b, preferred_element_type=F32)


def _silu(x):
    return x * jax.nn.sigmoid(x)


def _pack_halves(x16):
    half = x16.shape[1] // 2
    return _pack_pair(x16[:, :half], x16[:, half:])


def _pack_pair(lo16, hi16):
    lo = lax.bitcast_convert_type(lo16.astype(F32), jnp.uint32)
    hi = lax.bitcast_convert_type(hi16.astype(F32), jnp.uint32)
    return (lo >> 16) | hi


def _unpack_half(words, high):
    bits = jnp.where(high, words & jnp.uint32(0xFFFF0000), words << 16)
    return lax.bitcast_convert_type(bits, F32).astype(BF16)


def _pool_prompt_kernel(x_ref, w_ref, wp_ref, sc_ref, out_ref, st_ref, u_buf, a_buf, b_buf,
                        *, tm, windows, pool_buf):
    g = pl.program_id(0)
    t = pl.program_id(2)
    nt = pl.num_programs(2)
    halo = POOL_HALO
    rows = halo + tm

    @pl.when(t == 0)
    def _():
        u_buf[0:halo, :] = jnp.zeros((halo, u_buf.shape[1]), F32)

    u = _dot(x_ref[...], w_ref[...])
    u_buf[halo:rows, :] = u
    pos1 = (t * tm + lax.broadcasted_iota(jnp.int32, (tm, 1), 0) + 1).astype(F32)

    for gi, w in enumerate(windows):
        @pl.when(g == gi)
        def _(w=w):
            src, shift, start = u_buf, 1, SUBLANES
            dsts = (a_buf, b_buf)
            lev = 0
            while shift < w:
                dst = dsts[lev % 2]
                n = rows - start
                dst[start:rows, :] = src[start:rows, :] + src[start - shift:start - shift + n, :]
                src, shift, start, lev = dst, shift * 2, start + SUBLANES, lev + 1
            win = src[halo:rows, :]
            inv_cnt = 1.0 / jnp.minimum(pos1, float(w))
            pooled = (win * inv_cnt - u).astype(BF16)
            out_ref[...] = (_dot(pooled, wp_ref[...]) * sc_ref[...]).astype(out_ref.dtype)

    @pl.when(t == nt - 1)
    def _():
        st_ref[...] = u_buf[rows - pool_buf:rows, :]

    u_buf[0:halo, :] = u_buf[tm:rows, :]


def _pool_prompt(x16, w_in16, w_pool16, pool_scale, batch, seq, pool_buf):
    m, d = x16.shape
    ngrp, pg, _ = w_pool16.shape
    pw = ngrp * pg
    assert max(POOL_WINDOWS) <= POOL_HALO and len(POOL_WINDOWS) == ngrp and pg % LANES == 0
    tm = _pick_tile(seq, 1024, 16)
    nt = seq // tm
    kern = functools.partial(_pool_prompt_kernel, tm=tm, windows=POOL_WINDOWS, pool_buf=pool_buf)
    return pl.pallas_call(
        kern,
        grid=(ngrp, batch, nt),
        in_specs=[
            pl.BlockSpec((tm, d), lambda g, b, t: (b * nt + t, 0)),
            pl.BlockSpec((d, pg), lambda g, b, t: (0, g)),
            pl.BlockSpec((None, pg, pg), lambda g, b, t: (g, 0, 0)),
            pl.BlockSpec((1, pg), lambda g, b, t: (0, g)),
        ],
        out_specs=[
            pl.BlockSpec((tm, pg), lambda g, b, t: (b * nt + t, g)),
            pl.BlockSpec((None, pool_buf, pg), lambda g, b, t: (b, 0, g)),
        ],
        out_shape=[
            jax.ShapeDtypeStruct((m, pw), BF16),
            jax.ShapeDtypeStruct((batch, pool_buf, pw), F32),
        ],
        scratch_shapes=[pltpu.VMEM((POOL_HALO + tm, pg), F32)] * 3,
        compiler_params=_params("arbitrary", "arbitrary", "arbitrary"),
        name="pool_prompt",
    )(x16, w_in16, w_pool16, pool_scale)


def _conv_prompt_kernel(x_ref, wb_ref, wc_ref, wh_ref, cw_ref, out_ref, st_ref, v_buf,
                        *, tm, conv_buf):
    t = pl.program_id(2)
    nt = pl.num_programs(2)
    halo = CONV_HALO
    rows = halo + tm

    @pl.when(t == 0)
    def _():
        v_buf[0:halo, :] = jnp.zeros((halo, v_buf.shape[1]), F32)

    x = x_ref[...]
    v = _dot(x, wc_ref[...]) * _dot(x, wh_ref[...])
    v_buf[halo:rows, :] = v
    conv = (cw_ref[0:1, :] * v_buf[halo - 2:rows - 2, :]
            + cw_ref[1:2, :] * v_buf[halo - 1:rows - 1, :]
            + cw_ref[2:3, :] * v)
    out_ref[...] = (_dot(x, wb_ref[...]) * conv).astype(out_ref.dtype)

    @pl.when(t == nt - 1)
    def _():
        st_ref[...] = v_buf[rows - conv_buf:rows, :]

    v_buf[0:halo, :] = v_buf[tm:rows, :]


def _conv_prompt(x16, w_in16, conv_w, batch, seq, pw, cw, ct, conv_buf):
    m, d = x16.shape
    assert conv_w.shape[0] == 3 and conv_buf == 2 and cw % ct == 0 and pw % ct == 0
    nc = cw // ct
    off_b, off_c, off_h = pw // ct, (pw + cw) // ct, (pw + 2 * cw) // ct
    tm = _pick_tile(seq, 1024, 16)
    nt = seq // tm
    kern = functools.partial(_conv_prompt_kernel, tm=tm, conv_buf=conv_buf)
    return pl.pallas_call(
        kern,
        grid=(nc, batch, nt),
        in_specs=[
            pl.BlockSpec((tm, d), lambda c, b, t: (b * nt + t, 0)),
            pl.BlockSpec((d, ct), lambda c, b, t: (0, off_b + c)),
            pl.BlockSpec((d, ct), lambda c, b, t: (0, off_c + c)),
            pl.BlockSpec((d, ct), lambda c, b, t: (0, off_h + c)),
            pl.BlockSpec((3, ct), lambda c, b, t: (0, c)),
        ],
        out_specs=[
            pl.BlockSpec((tm, ct), lambda c, b, t: (b * nt + t, c)),
            pl.BlockSpec((None, conv_buf, ct), lambda c, b, t: (b, 0, c)),
        ],
        out_shape=[
            jax.ShapeDtypeStruct((m, cw), BF16),
            jax.ShapeDtypeStruct((batch, conv_buf, cw), F32),
        ],
        scratch_shapes=[pltpu.VMEM((CONV_HALO + tm, ct), F32)],
        compiler_params=_params("arbitrary", "arbitrary", "arbitrary"),
        name="conv_prompt",
    )(x16, w_in16, w_in16, w_in16, conv_w)


def _mixer_sample_kernel(x_ref, wu_ref, wb_ref, wc_ref, wh_ref, stp_ref, stc_ref, cw_ref, wp_ref,
                         sc_ref, pool_ref, conv_ref, u_ref, v_ref, *, windows, pool_buf):
    c = pl.program_id(0)
    x = x_ref[...]
    u = _dot(x, wu_ref[...])
    u_ref[...] = u
    for gi, w in enumerate(windows):
        @pl.when(c == gi)
        def _(w=w):
            win = u
            for j in range(1, w):
                win = win + stp_ref[pool_buf - j]
            cnt = float(min(PAST_LEN + 1, w))
            pooled = (win / cnt - u).astype(BF16)
            pool_ref[...] = (_dot(pooled, wp_ref[...]) * sc_ref[...]).astype(pool_ref.dtype)

    v = _dot(x, wc_ref[...]) * _dot(x, wh_ref[...])
    v_ref[...] = v
    conv = cw_ref[0:1, :] * stc_ref[0] + cw_ref[1:2, :] * stc_ref[1] + cw_ref[2:3, :] * v
    conv_ref[...] = (_dot(x, wb_ref[...]) * conv).astype(conv_ref.dtype)


def _mixer_sample(x16, w_in16, stp_t, stc_t, conv_w, w_pool16, pool_scale):
    bs, d = x16.shape
    ngrp, pg, _ = w_pool16.shape
    pool_buf, conv_buf = stp_t.shape[0], stc_t.shape[0]
    pw, cw = stp_t.shape[2], stc_t.shape[2]
    assert cw == pw, "decode mixer walks pooling and conv column tiles together"
    kern = functools.partial(_mixer_sample_kernel, windows=POOL_WINDOWS, pool_buf=pool_buf)
    col = lambda off: pl.BlockSpec((d, pg), lambda c: (0, off + c))
    tile = pl.BlockSpec((bs, pg), lambda c: (0, c))
    return pl.pallas_call(
        kern,
        grid=(ngrp,),
        in_specs=[
            pl.BlockSpec((bs, d), lambda c: (0, 0)),
            col(0), col(pw // pg), col((pw + cw) // pg), col((pw + 2 * cw) // pg),
            pl.BlockSpec((pool_buf, bs, pg), lambda c: (0, 0, c)),
            pl.BlockSpec((conv_buf, bs, pg), lambda c: (0, 0, c)),
            pl.BlockSpec((3, pg), lambda c: (0, c)),
            pl.BlockSpec((None, pg, pg), lambda c: (c, 0, 0)),
            pl.BlockSpec((1, pg), lambda c: (0, c)),
        ],
        out_specs=[tile, tile, tile, tile],
        out_shape=[
            jax.ShapeDtypeStruct((bs, pw), BF16),
            jax.ShapeDtypeStruct((bs, cw), BF16),
            jax.ShapeDtypeStruct((bs, pw), F32),
            jax.ShapeDtypeStruct((bs, cw), F32),
        ],
        compiler_params=_params("arbitrary"),
        name="mixer_sample",
    )(x16, w_in16, w_in16, w_in16, w_in16, stp_t, stc_t, conv_w, w_pool16, pool_scale)


def _layer_norm(r, g, b):
    mu = jnp.mean(r, axis=-1, keepdims=True)
    d = r - mu
    var = jnp.mean(d * d, axis=-1, keepdims=True)
    return d * lax.rsqrt(var + LN_EPS) * g + b


def _outproj_kernel(pool_ref, conv_ref, wo_ref, x_ref, g_ref, b_ref, wr_ref, *rest, alpha, nk_pool):
    h32_ref, h16_ref, hp_ref, sc_ref, acc = rest[-5:]
    k = pl.program_id(1)
    nk = pl.num_programs(1)

    @pl.when(k == 0)
    def _():
        acc[...] = _dot(pool_ref[...], wo_ref[...])

    @pl.when(jnp.logical_and(k > 0, k < nk_pool))
    def _():
        acc[...] += _dot(pool_ref[...], wo_ref[...])

    @pl.when(k >= nk_pool)
    def _():
        acc[...] += _dot(conv_ref[...], wo_ref[...])

    xc = x_ref.shape[1]
    cols = pl.ds(pl.multiple_of(k * xc, xc), xc)
    acc[:, cols] += alpha * x_ref[...]

    @pl.when(k == nk - 1)
    def _():
        h = _layer_norm(acc[...], g_ref[...], b_ref[...])
        h32_ref[...] = h
        h16 = h.astype(BF16)
        h16_ref[...] = h16
        hp_ref[...] = _pack_halves(h16)
        sc_ref[...] = jax.nn.sigmoid(_dot(h16, wr_ref[...]))


def _outproj(pool, conv, w_o16, x, ln_g, ln_b, w_r16, alpha, m_total, row0, fill=None):
    m, pw = pool.shape
    cw = conv.shape[1]
    d = w_o16.shape[1]
    e = w_r16.shape[1]
    tm = _pick_tile(math.gcd(m, row0) if row0 else m, 512, 16)
    tk = _pick_tile(math.gcd(pw, cw), 512, LANES)
    nkp, nkc = pw // tk, cw // tk
    xc = d // (nkp + nkc)
    assert xc % LANES == 0 and row0 % tm == 0 and row0 + m <= m_total
    blk0 = row0 // tm
    kern = functools.partial(_outproj_kernel, alpha=alpha, nk_pool=nkp)
    row = lambda width: pl.BlockSpec((tm, width), lambda i, k: (blk0 + i, 0), pipeline_mode=pl.Buffered(1))
    vec = pl.BlockSpec((1, d), lambda i, k: (0, 0))
    fill = () if fill is None else tuple(fill)
    n_in = 7
    return pl.pallas_call(
        kern,
        grid=(m // tm, nkp + nkc),
        in_specs=[
            pl.BlockSpec((tm, tk), lambda i, k: (i, jnp.minimum(k, nkp - 1))),
            pl.BlockSpec((tm, tk), lambda i, k: (i, jnp.maximum(k - nkp, 0))),
            pl.BlockSpec((tk, d), lambda i, k: (k, 0)),
            pl.BlockSpec((tm, xc), lambda i, k: (i, k)), vec, vec,
            pl.BlockSpec((d, e), lambda i, k: (0, 0)),
        ] + [pl.BlockSpec(memory_space=pl.ANY)] * len(fill),
        out_specs=[row(d), row(d), row(d // 2), row(e)],
        out_shape=[
            jax.ShapeDtypeStruct((m_total, d), F32),
            jax.ShapeDtypeStruct((m_total, d), BF16),
            jax.ShapeDtypeStruct((m_total, d // 2), jnp.uint32),
            jax.ShapeDtypeStruct((m_total, e), F32),
        ],
        input_output_aliases={n_in + j: j for j in range(len(fill))},
        scratch_shapes=[pltpu.VMEM((tm, d), F32)],
        compiler_params=_params("arbitrary", "arbitrary"),
        name="outproj_ln_router",
    )(pool, conv, w_o16, x, ln_g, ln_b, w_r16, *fill)


def _slot_row(a, m_tok):
    return (a & (TOP_K - 1)) * m_tok + _token_of(a)


def _token_of(a):
    return lax.shift_right_logical(a, TOP_K_SHIFT)


def _moe_kernel(order_ref, e_ref, ep_ref, st_ref, cnt_ref,
                hp_hbm, wg_ref, wu_ref, wd_ref,
                y_hbm,
                xbuf, gu_acc, hb16, ylow, ypack, gsem, ssem,
                *, ka, nb, tk, tn, de, m_tok):
    del e_ref, ep_ref
    i = pl.program_id(0)
    s = pl.program_id(1)
    ni = pl.num_programs(0)
    cnt = cnt_ref[i]
    slot = i % 2
    variant = (cnt + ROW_GRAIN - 1) // ROW_GRAIN

    def gather_copy(item_slot, j, tok):
        return pltpu.make_async_copy(hp_hbm.at[pl.ds(tok, 1), :], xbuf.at[item_slot, pl.ds(j, 1), :],
                                     gsem.at[item_slot])

    def scatter_copy(j, row):
        return pltpu.make_async_copy(ypack.at[pl.ds(j, 1), :], y_hbm.at[pl.ds(row, 1), :], ssem.at[0])

    def share(count, part, parts):
        chunk = (count + parts - 1) // parts
        lo = jnp.minimum(part * chunk, count)
        return lo, jnp.minimum(lo + chunk, count)

    def start_gather(item, item_slot, lo, hi):
        base = st_ref[item]

        def body(j, carry):
            gather_copy(item_slot, j, _token_of(order_ref[base + j])).start()
            return carry
        lax.fori_loop(lo, hi, body, 0)

    def start_scatter(item, lo, hi):
        base = st_ref[item]

        def body(j, carry):
            scatter_copy(j, _slot_row(order_ref[base + j], m_tok)).start()
            return carry
        lax.fori_loop(lo, hi, body, 0)

    def wait_rows(count, block_copy):
        p = ITEM_ROWS
        while p >= 1:
            pl.when((count & p) != 0)(functools.partial(lambda q: block_copy(q).wait(), p))
            p //= 2

    def wait_gather(item, item_slot):
        wait_rows(cnt_ref[item], lambda p: pltpu.make_async_copy(
            hp_hbm.at[pl.ds(0, p), :], xbuf.at[item_slot, pl.ds(0, p), :], gsem.at[item_slot]))

    def wait_scatter(item):
        wait_rows(cnt_ref[item], lambda p: pltpu.make_async_copy(
            ypack.at[pl.ds(0, p), :], y_hbm.at[pl.ds(0, p), :], ssem.at[0]))

    def for_variant(fn):
        for v in range(1, ITEM_ROWS // ROW_GRAIN + 1):
            pl.when(variant == v)(functools.partial(fn, v * ROW_GRAIN))

    @pl.when(s == 0)
    def _():
        @pl.when(i == 0)
        def _():
            xbuf[...] = jnp.zeros(xbuf.shape, xbuf.dtype)
            start_gather(0, 0, 0, cnt)

        wait_gather(i, slot)

    @pl.when(i + 1 < ni)
    def _():
        lo, hi = share(cnt_ref[i + 1], s, ka + nb)
        start_gather(i + 1, 1 - slot, lo, hi)

    @pl.when(jnp.logical_and(s < ka, i > 0))
    def _():
        prev = jnp.maximum(i - 1, 0)
        lo, hi = share(cnt_ref[prev], s, ka)
        start_scatter(prev, lo, hi)

    @pl.when(s < ka)
    def _():
        chunks_per_half = ka // 2
        high = s >= chunks_per_half
        coff = pl.multiple_of((s % chunks_per_half) * tk, tk)

        def gate_up(rows):
            x16 = _unpack_half(xbuf[slot, 0:rows, pl.ds(coff, tk)], high)
            g = _dot(x16, wg_ref[...].astype(BF16))
            u = _dot(x16, wu_ref[...].astype(BF16))

            @pl.when(s == 0)
            def _():
                gu_acc[0:rows, 0:de] = g
                gu_acc[0:rows, de:2 * de] = u

            @pl.when(s > 0)
            def _():
                gu_acc[0:rows, 0:de] += g
                gu_acc[0:rows, de:2 * de] += u
        for_variant(gate_up)

    @pl.when(s >= ka)
    def _():
        n = s - ka
        half_blks = nb // 2
        woff = pl.multiple_of((n % half_blks) * tn, tn)

        @pl.when(n == 0)
        def _():
            @pl.when(i > 0)
            def _():
                wait_scatter(jnp.maximum(i - 1, 0))

            def act(rows):
                hb16[0:rows, :] = (_silu(gu_acc[0:rows, 0:de]) * gu_acc[0:rows, de:2 * de]).astype(BF16)
            for_variant(act)

        def down(rows):
            y = _dot(hb16[0:rows, :], wd_ref[...].astype(BF16))

            @pl.when(n < half_blks)
            def _():
                ylow[0:rows, pl.ds(woff, tn)] = y

            @pl.when(n >= half_blks)
            def _():
                ypack[0:rows, pl.ds(woff, tn)] = _pack_pair(ylow[0:rows, pl.ds(woff, tn)].astype(BF16),
                                                             y.astype(BF16))
        for_variant(down)

        @pl.when(jnp.logical_and(n == nb - 1, i == ni - 1))
        def _():
            start_scatter(i, 0, cnt)
            wait_scatter(i)


def _moe_routed(hp, order, n_items, item_e, item_eprev, item_start, item_cnt, w_gate, w_up, w_down, layer):
    m_tok, dh = hp.shape
    d = 2 * dh
    de = w_gate.shape[-1]
    tk = _pick_tile(dh, min(1024, d // 4), LANES)
    tn = tk
    ka, nb = d // tk, d // tn
    assert de % LANES == 0 and nb >= 2 and ka % 2 == 0 and nb % 2 == 0

    def wgu_map(i, s, order, e, ep, st, cnt):
        return (layer, e[i], jnp.minimum(s, ka - 1), 0)

    def wd_map(i, s, order, e, ep, st, cnt):
        first = s < ka
        return (layer, jnp.where(first, ep[i], e[i]), 0, jnp.where(first, nb - 1, s - ka))

    kern = functools.partial(_moe_kernel, ka=ka, nb=nb, tk=tk, tn=tn, de=de, m_tok=m_tok)
    grid_spec = pltpu.PrefetchScalarGridSpec(
        num_scalar_prefetch=5,
        grid=(n_items[0], ka + nb),
        in_specs=[
            pl.BlockSpec(memory_space=pl.ANY),
            pl.BlockSpec((None, None, tk, de), wgu_map),
            pl.BlockSpec((None, None, tk, de), wgu_map),
            pl.BlockSpec((None, None, de, tn), wd_map),
        ],
        out_specs=pl.BlockSpec(memory_space=pl.ANY),
        scratch_shapes=[
            pltpu.VMEM((2, ITEM_ROWS, dh), jnp.uint32),
            pltpu.VMEM((ITEM_ROWS, 2 * de), F32),
            pltpu.VMEM((ITEM_ROWS, de), BF16),
            pltpu.VMEM((ITEM_ROWS, dh), F32),
            pltpu.VMEM((ITEM_ROWS, dh), jnp.uint32),
            pltpu.SemaphoreType.DMA((2,)),
            pltpu.SemaphoreType.DMA((1,)),
        ],
    )
    return pl.pallas_call(
        kern,
        grid_spec=grid_spec,
        out_shape=jax.ShapeDtypeStruct((TOP_K * m_tok, dh), jnp.uint32),
        compiler_params=_params("arbitrary", "arbitrary"),
        name="moe_routed",
    )(order, item_e, item_eprev, item_start, item_cnt, hp, w_gate, w_up, w_down)


def _route(scores, bias, n_experts):
    m = scores.shape[0]
    a = m * TOP_K
    _, idx = lax.top_k(scores + bias[None, :].astype(F32), TOP_K)
    sel = jnp.take_along_axis(scores, idx, axis=1)
    gates = sel / jnp.sum(sel, axis=-1, keepdims=True) * ROUTED_SCALE

    abits = max(1, (a - 1).bit_length())
    assert (n_experts << abits) < 2 ** 31
    e_flat = idx.reshape(-1).astype(jnp.int32)
    keys = lax.sort((e_flat << abits) | jnp.arange(a, dtype=jnp.int32))
    order = keys & ((1 << abits) - 1)
    experts = jnp.arange(n_experts, dtype=jnp.int32)
    counts = jnp.sum((idx.astype(jnp.int32)[:, :, None] == experts[None, None, :]).astype(jnp.int32), axis=(0, 1))
    starts = jnp.cumsum(counts) - counts

    ni = n_experts + a // ITEM_ROWS
    n_items = (counts + ITEM_ROWS - 1) // ITEM_ROWS
    item_end = jnp.cumsum(n_items)
    total = item_end[-1]
    slots = jnp.arange(ni, dtype=jnp.int32)
    live = slots < total
    sl = jnp.minimum(slots, total - 1)
    ex = jnp.sum((item_end[None, :] <= sl[:, None]).astype(jnp.int32), axis=1)
    jj = sl - (item_end[ex] - n_items[ex])
    item_start = starts[ex] + jj * ITEM_ROWS
    item_cnt = jnp.where(live, jnp.minimum(ITEM_ROWS, counts[ex] - jj * ITEM_ROWS), 0)
    ex_prev = jnp.concatenate([ex[:1], ex[:-1]])
    return (gates, order.astype(jnp.int32), total.reshape(1).astype(jnp.int32), ex.astype(jnp.int32),
            ex_prev.astype(jnp.int32), item_start.astype(jnp.int32), item_cnt.astype(jnp.int32))


def _shared_kernel(h_ref, wgu_ref, wd_ref, out_ref, *, ch):
    c = pl.program_id(1)
    gu = _dot(h_ref[...], wgu_ref[...])
    hb = (_silu(gu[:, 0:ch]) * gu[:, ch:2 * ch]).astype(BF16)
    part = _dot(hb, wd_ref[...])

    @pl.when(c == 0)
    def _():
        out_ref[...] = part

    @pl.when(c > 0)
    def _():
        out_ref[...] += part


def _shared_ffn(h16, ws_gate16, ws_up16, ws_down16):
    m, d = h16.shape
    de = ws_gate16.shape[1]
    ch = _pick_tile(de, 384, LANES)
    nc = de // ch
    tm = _pick_tile(m, 416, 16)
    wgu = jnp.concatenate(
        [jnp.concatenate([ws_gate16[:, c * ch:(c + 1) * ch], ws_up16[:, c * ch:(c + 1) * ch]], axis=1)
         for c in range(nc)], axis=1)
    return pl.pallas_call(
        functools.partial(_shared_kernel, ch=ch),
        grid=(m // tm, nc),
        in_specs=[
            pl.BlockSpec((tm, d), lambda i, c: (i, 0)),
            pl.BlockSpec((d, 2 * ch), lambda i, c: (0, c)),
            pl.BlockSpec((ch, d), lambda i, c: (c, 0)),
        ],
        out_specs=pl.BlockSpec((tm, d), lambda i, c: (i, 0)),
        out_shape=jax.ShapeDtypeStruct((m, d), F32),
        compiler_params=_params("arbitrary", "arbitrary"),
        name="shared_ffn",
    )(h16, wgu, ws_down16)


def _combine_ln_kernel(h_ref, *refs, alpha, n_prompt):
    y_refs = refs[:TOP_K]
    gt_ref, s_ref, g_ref, b_ref, outp_ref, outs_ref = refs[TOP_K:]
    i = pl.program_id(0)
    lo = hi = None
    for k in range(TOP_K):
        words = y_refs[k][...]
        gate = gt_ref[:, k:k + 1]
        lo_k = gate * lax.bitcast_convert_type(words << 16, F32)
        hi_k = gate * lax.bitcast_convert_type(words & jnp.uint32(0xFFFF0000), F32)
        lo = lo_k if lo is None else lo + lo_k
        hi = hi_k if hi is None else hi + hi_k
    routed = jnp.concatenate([lo, hi], axis=1)
    y = _layer_norm(alpha * h_ref[...] + (routed + s_ref[...]), g_ref[...], b_ref[...])

    @pl.when(i < n_prompt)
    def _():
        outp_ref[...] = y

    @pl.when(i >= n_prompt)
    def _():
        outs_ref[...] = y


def _combine_ln(h32, y_slots, gates, shared, ln_g, ln_b, alpha, m_prompt):
    m, d = h32.shape
    m_sample = m - m_prompt
    tm = _pick_tile(math.gcd(m_prompt, m_sample), 128, SUBLANES)
    n_prompt = m_prompt // tm
    nt = m // tm
    row = pl.BlockSpec((tm, d), lambda i: (i, 0))
    vec = pl.BlockSpec((1, d), lambda i: (0, 0))
    slot_rows = [pl.BlockSpec((tm, d // 2), functools.partial(lambda k, i: (k * nt + i, 0), k))
                 for k in range(TOP_K)]
    return pl.pallas_call(
        functools.partial(_combine_ln_kernel, alpha=alpha, n_prompt=n_prompt),
        grid=(nt,),
        in_specs=[row, *slot_rows, pl.BlockSpec((tm, TOP_K), lambda i: (i, 0)), row, vec, vec],
        out_specs=[pl.BlockSpec((tm, d), lambda i: (jnp.minimum(i, n_prompt - 1), 0)),
                   pl.BlockSpec((tm, d), lambda i: (jnp.maximum(i - n_prompt, 0), 0))],
        out_shape=[jax.ShapeDtypeStruct((m_prompt, d), F32), jax.ShapeDtypeStruct((m_sample, d), F32)],
        compiler_params=_params("arbitrary"),
        name="combine_ln",
    )(h32, *([y_slots] * TOP_K), gates, shared, ln_g, ln_b)


def _layer(xp, xs, st_pool, st_conv, alpha, layer, w_in, w_pool, pool_scale, conv_w, w_o,
           ln1_g, ln1_b, ln2_g, ln2_b, w_router, router_bias,
           w_gate, w_up, w_down, ws_gate, ws_up, ws_down):
    batch, seq, d = xp.shape
    bs = xs.shape[0]
    pool_buf, pw = st_pool.shape[1], st_pool.shape[2]
    conv_buf, cw = st_conv.shape[1], st_conv.shape[2]
    n_experts = w_router.shape[1]
    pg = w_pool.shape[1]

    xp2 = xp.reshape(batch * seq, d)
    xs2 = xs.reshape(bs, d)
    w_in16 = w_in.astype(BF16)
    w_pool16 = w_pool.astype(BF16)
    scale2 = pool_scale.reshape(1, pw)

    xp16 = xp2.astype(BF16)
    pool_p, new_pool_p = _pool_prompt(xp16, w_in16, w_pool16, scale2, batch, seq, pool_buf)
    conv_p, new_conv_p = _conv_prompt(xp16, w_in16, conv_w, batch, seq, pw, cw, pg, conv_buf)

    stp_t = jnp.transpose(st_pool, (1, 0, 2))
    stc_t = jnp.transpose(st_conv, (1, 0, 2))
    pool_s, conv_s, u_s, v_s = _mixer_sample(xs2.astype(BF16), w_in16, stp_t, stc_t, conv_w,
                                             w_pool16, scale2)
    new_pool_s = jnp.concatenate([st_pool[:, 1:], u_s[:, None, :]], axis=1)
    new_conv_s = jnp.concatenate([st_conv[:, 1:], v_s[:, None, :]], axis=1)

    m_prompt, m_all = batch * seq, batch * seq + bs
    w_o16, w_r16 = w_o.astype(BF16), w_router.astype(BF16)
    g1, b1 = ln1_g.reshape(1, d), ln1_b.reshape(1, d)
    outs = _outproj(pool_p, conv_p, w_o16, xp2, g1, b1, w_r16, alpha, m_all, 0)
    h32, h16, hp, scores = _outproj(pool_s, conv_s, w_o16, xs2, g1, b1, w_r16, alpha, m_all, m_prompt,
                                    fill=outs)

    gates, order, n_items, item_e, item_ep, item_start, item_cnt = _route(scores, router_bias, n_experts)
    y_slots = _moe_routed(hp, order, n_items, item_e, item_ep, item_start, item_cnt,
                          w_gate, w_up, w_down, layer)
    shared = _shared_ffn(h16, ws_gate.astype(BF16), ws_up.astype(BF16), ws_down.astype(BF16))
    yp, ys = _combine_ln(h32, y_slots, gates, shared,
                         ln2_g.reshape(1, d), ln2_b.reshape(1, d), alpha, batch * seq)
    return (yp.reshape(batch, seq, d), ys.reshape(bs, xs.shape[1], d),
            new_pool_p, new_pool_s, new_conv_p, new_conv_s)


def kernel(x_prompt, x_sample, state_pool, state_conv, w_in, w_pool, pool_scale, conv_w, w_o,
           ln1_g, ln1_b, ln2_g, ln2_b, w_router, router_bias,
           w_gate, w_up, w_down, ws_gate, ws_up, ws_down):
    depth = w_in.shape[0]
    assert x_sample.shape[1] == 1, "decode rows carry one new token per sequence"
    alpha = (2.0 * depth) ** 0.25
    hp, hs = x_prompt, x_sample
    pool_p, pool_s, conv_p, conv_s = [], [], [], []
    for l in range(depth):
        hp, hs, npp, nps, ncp, ncs = _layer(
            hp, hs, state_pool[l], state_conv[l], alpha, l, w_in[l], w_pool[l], pool_scale[l],
            conv_w[l], w_o[l], ln1_g[l], ln1_b[l], ln2_g[l], ln2_b[l], w_router[l],
            router_bias[l], w_gate, w_up, w_down, ws_gate[l], ws_up[l], ws_down[l])
        pool_p.append(npp)
        pool_s.append(nps)
        conv_p.append(ncp)
        conv_s.append(ncs)
    return (hp, hs, jnp.stack(pool_p, 0), jnp.stack(pool_s, 0), jnp.stack(conv_p, 0), jnp.stack(conv_s, 0))
```
